```python
import jax, jax.numpy as jnp
from jax import lax
import numpy as np

D_MODEL = 1024
BATCH = 8
SEQ = 8192
DEPTH = 1

CTX_LEN = 256
GRID_W = 64
CHUNK = 64
EPS = 1e-6

GLA_HEADS = 4
GLA_QK = D_MODEL // 2
GLA_V = D_MODEL
GLA_DK = GLA_QK // GLA_HEADS
GLA_DV = GLA_V // GLA_HEADS
GLA_LOWRANK = 16
GLA_GATE_NORM = 16.0

GDN_DK = 128
GDN_DV = 128
GDN_HEADS = D_MODEL // GDN_DV
GDN_QK = GDN_HEADS * GDN_DK
GDN_V = GDN_HEADS * GDN_DV
CONV_W = 5

_FFN_RAW = -(-8 * D_MODEL // 3)
FFN_HIDDEN = -(-_FFN_RAW // 256) * 256

SPLITS = (GLA_QK, GLA_QK, GLA_V, GLA_V, GLA_LOWRANK, GLA_LOWRANK,
          GDN_QK, GDN_QK, GDN_V, GDN_V, 4 * GDN_HEADS,
          D_MODEL, D_MODEL)
P_IN = sum(SPLITS)

kernel_name = 'hybrid_gla_gdn_prefix_dit_block'


def rmsnorm(t, w):
    tf = t.astype(jnp.float32)
    y = tf * lax.rsqrt(jnp.mean(tf * tf, axis=-1, keepdims=True) + EPS)
    return y.astype(t.dtype) * w


def l2norm(t):
    tf = t.astype(jnp.float32)
    return (tf * lax.rsqrt(jnp.sum(tf * tf, axis=-1, keepdims=True) + EPS)).astype(t.dtype)


def modulate(t, shift, scale):
    return t * (1 + scale) + shift


def split_cols(p):
    points = [int(s) for s in np.cumsum(SPLITS)[:-1]]
    return jnp.split(p, points, axis=-1)


def to_heads(t, n_heads):
    B, L, W = t.shape
    return t.reshape(B, L, n_heads, W // n_heads).transpose(0, 2, 1, 3)


def from_heads(t):
    B, H, L, d = t.shape
    return t.transpose(0, 2, 1, 3).reshape(B, L, H * d)


def to_chunks(t):
    B, H, L = t.shape[:3]
    t = t.reshape(B, H, L // CHUNK, CHUNK, *t.shape[3:])
    return jnp.moveaxis(t, 2, 0)


def from_chunks(t):
    t = jnp.moveaxis(t, 0, 2)
    B, H, n, C = t.shape[:4]
    return t.reshape(B, H, n * C, *t.shape[4:])


def to_column_major(t):
    B, L, C = t.shape
    rows = L // GRID_W
    return t.reshape(B, rows, GRID_W, C).transpose(0, 2, 1, 3)


def from_column_major(t):
    B, L, C = t.shape
    rows = L // GRID_W
    return t.reshape(B, GRID_W, rows, C).transpose(0, 2, 1, 3).reshape(B, L, C)


def centred_dwconv(t, w):
    pad = CONV_W // 2
    L = t.shape[-2]
    tp = jnp.pad(t, [(0, 0)] * (t.ndim - 2) + [(pad, pad), (0, 0)])
    return sum(w[i] * tp[..., i:i + L, :] for i in range(CONV_W))


def gla_scan(q, k, v, g, s0):
    out_dtype = v.dtype
    q, k, v, g = (t.astype(jnp.float32) for t in (q, k, v, g))
    incl = jnp.tril(jnp.ones((CHUNK, CHUNK), dtype=bool))

    def step(S, inp):
        qc, kc, vc, gc = inp
        G = jnp.cumsum(gc, axis=-2)
        diff = G[..., :, None, :] - G[..., None, :, :]
        decay = jnp.exp(jnp.where(incl[:, :, None], diff, -jnp.inf))
        A = jnp.einsum('bhid,bhjd,bhijd->bhij', qc, kc, decay)
        o = (jnp.einsum('bhij,bhjv->bhiv', A, vc)
             + jnp.einsum('bhid,bhdv->bhiv', qc * jnp.exp(G), S))
        G_end = G[..., -1, :]
        S = (jnp.exp(G_end)[..., None] * S
             + jnp.einsum('bhjd,bhjv->bhdv', kc * jnp.exp(G_end[..., None, :] - G), vc))
        return S, o

    S, o = lax.scan(step, s0.astype(jnp.float32), tuple(to_chunks(t) for t in (q, k, v, g)))
    return S, from_chunks(o).astype(out_dtype)


def gdn_scan(q, k, v, g, beta, s0):
    out_dtype = v.dtype
    q, k, v, g, beta = (to_chunks(t.astype(jnp.float32)) for t in (q, k, v, g, beta))
    strict = jnp.tril(jnp.ones((CHUNK, CHUNK), dtype=bool), -1)
    incl = jnp.tril(jnp.ones((CHUNK, CHUNK), dtype=bool))
    G = jnp.cumsum(g, axis=-1)
    diff = G[..., :, None] - G[..., None, :]
    kk = jnp.einsum('nbhid,nbhjd->nbhij', k, k)
    Lmat = beta[..., :, None] * kk * jnp.exp(jnp.where(strict, diff, -jnp.inf))
    T = Lmat + jnp.eye(CHUNK, dtype=jnp.float32)
    u = lax.linalg.triangular_solve(T, beta[..., None] * v, left_side=True, lower=True, unit_diagonal=True)
    w = lax.linalg.triangular_solve(T, (beta * jnp.exp(G))[..., None] * k, left_side=True, lower=True, unit_diagonal=True)
    Aqk = jnp.einsum('nbhid,nbhjd->nbhij', q, k) * jnp.exp(jnp.where(incl, diff, -jnp.inf))
    q_dec = q * jnp.exp(G)[..., None]
    G_end = G[..., -1]
    k_dec = k * jnp.exp(G_end[..., None] - G)[..., None]

    def step(S, inp):
        u_c, w_c, A_c, qd, kd, ge = inp
        v_new = u_c - jnp.einsum('bhid,bhdv->bhiv', w_c, S)
        o = jnp.einsum('bhid,bhdv->bhiv', qd, S) + jnp.einsum('bhij,bhjv->bhiv', A_c, v_new)
        S = jnp.exp(ge)[..., None, None] * S + jnp.einsum('bhjd,bhjv->bhdv', kd, v_new)
        return S, o

    S, o = lax.scan(step, s0.astype(jnp.float32), (u, w, Aqk, q_dec, k_dec, G_end))
    return S, from_chunks(o).astype(out_dtype)


def _identity(t):
    return t


def _reverse(t):
    return jnp.flip(t, axis=2)


def bidirectional(scan_fn, ctx_qkv, ctx_gates, lat_qkv, lat_gates, s0):
    outs_c, outs_l = [], []
    for direction, rev in enumerate((_identity, _reverse)):
        s_ctx, o_c = scan_fn(*(rev(t) for t in ctx_qkv), *(rev(t) for t in ctx_gates[direction]), s0)
        _, o_l = scan_fn(*(rev(t) for t in lat_qkv), *(rev(t) for t in lat_gates[direction]), s_ctx)
        outs_c.append(rev(o_c))
        outs_l.append(rev(o_l))
    return outs_c[0] + outs_c[1], outs_l[0] + outs_l[1]


def gla_prepare(q, k, v, lr_f, lr_b, lr_w, lr_bias):
    q = to_heads(q * GLA_DK ** -0.5, GLA_HEADS)
    k = to_heads(k, GLA_HEADS)
    v = to_heads(v, GLA_HEADS)
    gates = tuple(
        (to_heads(jax.nn.log_sigmoid((lr @ lr_w[d] + lr_bias[d]).astype(jnp.float32)) / GLA_GATE_NORM, GLA_HEADS),)
        for d, lr in enumerate((lr_f, lr_b)))
    return (q, k, v), gates


def gdn_prepare(q, k, v, ab, conv_w, a_log, dt_bias, column_major):
    B, L, _ = q.shape
    qkv = jnp.concatenate([q, k, v], axis=-1)
    if column_major:
        qkv = centred_dwconv(to_column_major(qkv), conv_w).reshape(B, L, -1)
        ab = to_column_major(ab).reshape(B, L, -1)
    else:
        qkv = centred_dwconv(qkv, conv_w)
    qkv = jax.nn.silu(qkv)
    q, k, v = jnp.split(qkv, [GDN_QK, 2 * GDN_QK], axis=-1)
    q = l2norm(to_heads(q, GDN_HEADS)) * GDN_DK ** -0.5
    k = l2norm(to_heads(k, GDN_HEADS))
    v = to_heads(v, GDN_HEADS)
    a_f, a_b, b_f, b_b = jnp.split(ab, 4, axis=-1)
    gates = tuple(
        ((-jnp.exp(a_log[d]) * jax.nn.softplus(a.astype(jnp.float32) + dt_bias[d])).transpose(0, 2, 1),
         jax.nn.sigmoid(b.astype(jnp.float32)).transpose(0, 2, 1))
        for d, (a, b) in enumerate(((a_f, b_f), (a_b, b_b))))
    return (q, k, v), gates


def head_norm_gate(o, gain, z):
    B, L, _ = o.shape
    H, dv = gain.shape
    of = o.astype(jnp.float32).reshape(B, L, H, dv)
    y = (of * lax.rsqrt(jnp.mean(of * of, axis=-1, keepdims=True) + EPS)).astype(z.dtype) * gain
    return y.reshape(B, L, H * dv) * jax.nn.silu(z)


def token_mixers(h_c, h_l, w_in, lr_w, lr_bias, gla_gain, conv_w, a_log, dt_bias, gdn_gain, w_out, need_ctx):
    B = h_l.shape[0]
    pc = split_cols(h_c @ w_in)
    pl = split_cols(h_l @ w_in)
    gc_qkv, gc_g = gla_prepare(pc[0], pc[1], pc[2], pc[4], pc[5], lr_w, lr_bias)
    gl_qkv, gl_g = gla_prepare(pl[0], pl[1], pl[2], pl[4], pl[5], lr_w, lr_bias)
    s0_gla = jnp.zeros((B, GLA_HEADS, GLA_DK, GLA_DV), jnp.float32)
    o_gla_c, o_gla_l = bidirectional(gla_scan, gc_qkv, gc_g, gl_qkv, gl_g, s0_gla)
    dc_qkv, dc_g = gdn_prepare(pc[6], pc[7], pc[8], pc[10], conv_w, a_log, dt_bias, False)
    dl_qkv, dl_g = gdn_prepare(pl[6], pl[7], pl[8], pl[10], conv_w, a_log, dt_bias, True)
    s0_gdn = jnp.zeros((B, GDN_HEADS, GDN_DK, GDN_DV), jnp.float32)
    o_gdn_c, o_gdn_l = bidirectional(gdn_scan, dc_qkv, dc_g, dl_qkv, dl_g, s0_gdn)
    o_gdn_l = from_column_major(from_heads(o_gdn_l))

    def merge(p, o_gla, o_gdn):
        y_gla = head_norm_gate(o_gla, gla_gain, p[3])
        y_gdn = head_norm_gate(o_gdn, gdn_gain, p[9])
        return (jax.nn.sigmoid(p[11]) * y_gla + jax.nn.sigmoid(p[12]) * y_gdn) @ w_out

    y_l = merge(pl, from_heads(o_gla_l), o_gdn_l)
    y_c = merge(pc, from_heads(o_gla_c), from_heads(o_gdn_c)) if need_ctx else None
    return y_c, y_l


def swiglu(t, w_gate, w_up, w_down):
    return (jax.nn.silu(t @ w_gate) * (t @ w_up)) @ w_down


def setup_inputs(seed: int = 0) -> dict:
    key = jax.random.key(seed)
    ks = jax.random.split(key, 24)
    f32 = jnp.float32

    def nrm(k, shape, fan_in):
        return jax.random.normal(k, shape, f32) * fan_in ** -0.5

    def gain(k, shape):
        return 1.0 + 0.02 * jax.random.normal(k, shape, f32)

    dt = jnp.exp(jax.random.uniform(ks[14], (DEPTH, 2, GDN_HEADS), f32, np.log(1e-3), np.log(1e-1)))
    return {
        'x': jax.random.normal(ks[0], (BATCH, SEQ, D_MODEL), f32),
        'c': jax.random.normal(ks[1], (BATCH, D_MODEL), f32),
        'ctx': jax.random.normal(ks[2], (BATCH, CTX_LEN, D_MODEL), f32),
        'c_ctx': jax.random.normal(ks[3], (D_MODEL,), f32),
        'w_mod': nrm(ks[4], (DEPTH, D_MODEL, 6 * D_MODEL), D_MODEL),
        'b_mod': 0.02 * jax.random.normal(ks[5], (DEPTH, 6 * D_MODEL), f32),
        'norm1_w': gain(ks[6], (DEPTH, D_MODEL)),
        'norm2_w': gain(ks[7], (DEPTH, D_MODEL)),
        'w_in': nrm(ks[8], (DEPTH, D_MODEL, P_IN), D_MODEL),
        'gla_lr_w': nrm(ks[9], (DEPTH, 2, GLA_LOWRANK, GLA_QK), GLA_LOWRANK),
        'gla_lr_b': 0.1 * jax.random.normal(ks[10], (DEPTH, 2, GLA_QK), f32),
        'gla_norm_w': gain(ks[11], (DEPTH, GLA_HEADS, GLA_DV)),
        'gdn_conv_w': nrm(ks[12], (DEPTH, CONV_W, 2 * GDN_QK + GDN_V), CONV_W),
        'gdn_a_log': jnp.log(jax.random.uniform(ks[13], (DEPTH, 2, GDN_HEADS), f32, 1.0, 16.0)),
        'gdn_dt_bias': dt + jnp.log(-jnp.expm1(-dt)),
        'gdn_norm_w': gain(ks[15], (DEPTH, GDN_HEADS, GDN_DV)),
        'w_out': nrm(ks[16], (DEPTH, D_MODEL, D_MODEL), D_MODEL),
        'ffn_w_gate': nrm(ks[17], (DEPTH, D_MODEL, FFN_HIDDEN), D_MODEL),
        'ffn_w_up': nrm(ks[18], (DEPTH, D_MODEL, FFN_HIDDEN), D_MODEL),
        'ffn_w_down': nrm(ks[19], (DEPTH, FFN_HIDDEN, D_MODEL), FFN_HIDDEN),
        'final_norm_w': gain(ks[20], (D_MODEL,)),
    }


def reference(x, c, ctx, c_ctx, w_mod, b_mod, norm1_w, norm2_w, w_in, gla_lr_w, gla_lr_b, gla_norm_w,
              gdn_conv_w, gdn_a_log, gdn_dt_bias, gdn_norm_w, w_out, ffn_w_gate, ffn_w_up, ffn_w_down,
              final_norm_w):
    h = ctx
    for layer in range(DEPTH):
        need_ctx = layer < DEPTH - 1
        mod_l = jnp.split((jax.nn.silu(c) @ w_mod[layer] + b_mod[layer])[:, None, :], 6, axis=-1)
        mod_c = jnp.split((jax.nn.silu(c_ctx) @ w_mod[layer] + b_mod[layer])[None, None, :], 6, axis=-1)
        hn_l = modulate(rmsnorm(x, norm1_w[layer]), mod_l[0], mod_l[1])
        hn_c = modulate(rmsnorm(h, norm1_w[layer]), mod_c[0], mod_c[1])
        y_c, y_l = token_mixers(hn_c, hn_l, w_in[layer], gla_lr_w[layer], gla_lr_b[layer], gla_norm_w[layer],
                                gdn_conv_w[layer], gdn_a_log[layer], gdn_dt_bias[layer], gdn_norm_w[layer],
                                w_out[layer], need_ctx)
        x = x + mod_l[2] * y_l
        x = x + mod_l[5] * swiglu(modulate(rmsnorm(x, norm2_w[layer]), mod_l[3], mod_l[4]),
                                  ffn_w_gate[layer], ffn_w_up[layer], ffn_w_down[layer])
        if need_ctx:
            h = h + mod_c[2] * y_c
            h = h + mod_c[5] * swiglu(modulate(rmsnorm(h, norm2_w[layer]), mod_c[3], mod_c[4]),
                                      ffn_w_gate[layer], ffn_w_up[layer], ffn_w_down[layer])
    return rmsnorm(x, final_norm_w)
```

```python
import functools

import numpy as np
import jax
import jax.numpy as jnp
from jax import lax
from jax.experimental import pallas as pl
from jax.experimental.pallas import tpu as pltpu

F32 = jnp.float32
BF16 = jnp.bfloat16
HI = lax.Precision.HIGHEST

EPS = 1e-6
CHUNK = 64
GRID_W = 64
CONV_W = 5
CONV_PAD = 8

GLA_HEADS = 4
GLA_DK = 128
GLA_DV = 256
GLA_LOWRANK = 16
GLA_GATE_NORM = 16.0
GDN_HEADS = 8
GDN_D = 128

LANES = 128
VMEM_LIMIT = 56 * 1024 * 1024

SM_LRF, SM_LRB, SM_A, SM_B = 0, 16, 32, 48


def _dot(a, b, **kw):
    return jnp.dot(a, b, preferred_element_type=F32, **kw)


def _dot_nt(a, b):
    return lax.dot_general(a, b, (((1,), (1,)), ((), ())), preferred_element_type=F32)


def _dot_tn(a, b):
    return lax.dot_general(a, b, (((0,), (0,)), ((), ())), preferred_element_type=F32)


def _sigmoid(t):
    return 1.0 / (1.0 + jnp.exp(-t))


def _silu(t):
    return t * _sigmoid(t)


def _softplus(t):
    return jnp.maximum(t, 0.0) + jnp.log(1.0 + jnp.exp(-jnp.abs(t)))


def _rms(t, w):
    return t * lax.rsqrt(jnp.mean(t * t, axis=-1, keepdims=True) + EPS) * w


def _params(n_grid):
    return pltpu.CompilerParams(dimension_semantics=("arbitrary",) * n_grid,
                                vmem_limit_bytes=VMEM_LIMIT)


def _resident(shape):
    return pl.BlockSpec(shape, lambda *_: (0,) * len(shape), pipeline_mode=pl.Buffered(1))


def _mod_kernel(c_ref, w_ref, b_ref, o_ref):
    o_ref[...] = _dot(_silu(c_ref[...]), w_ref[...], precision=HI) + b_ref[...]


def _mod_call(cc, w_mod, b_mod):
    rows, d = cc.shape
    n = w_mod.shape[1]
    return pl.pallas_call(
        _mod_kernel,
        out_shape=jax.ShapeDtypeStruct((rows, n), F32),
        grid=(n // d,),
        in_specs=[pl.BlockSpec((rows, d), lambda j: (0, 0)),
                  pl.BlockSpec((d, d), lambda j: (0, j)),
                  pl.BlockSpec((1, d), lambda j: (0, j))],
        out_specs=pl.BlockSpec((rows, d), lambda j: (0, j)),
        compiler_params=_params(1),
        name="mod",
    )(cc, w_mod, b_mod)


_PROJ_GROUPS = (("gqk", 1024), ("gv", 1024), ("gz", 1024), ("dqkv", 3072), ("dz", 1024),
                ("og", 2048), ("sm", LANES))


def _inproj_kernel(x_ref, shift_ref, scale_ref, nw_ref, w_ref, *out_refs):
    h = _rms(x_ref[...], nw_ref[...]) * (1.0 + scale_ref[...]) + shift_ref[...]
    hb = h.astype(BF16)
    lo = 0
    for ref, (_, width) in zip(out_refs, _PROJ_GROUPS):
        ref[...] = _dot(hb, w_ref[:, lo:lo + width])
        lo += width


def _inproj_call(x2d, mod3, norm_w, w_perm, tokens_per_mod, tm):
    n, d = x2d.shape
    p = w_perm.shape[1]
    mod_row = lambda i: (i * tm) // tokens_per_mod
    return pl.pallas_call(
        _inproj_kernel,
        out_shape=[jax.ShapeDtypeStruct((n, w), F32) for _, w in _PROJ_GROUPS],
        grid=(n // tm,),
        in_specs=[pl.BlockSpec((tm, d), lambda i: (i, 0)),
                  pl.BlockSpec((None, 1, d), lambda i: (mod_row(i), 0, 0)),
                  pl.BlockSpec((None, 1, d), lambda i: (mod_row(i), 0, 1)),
                  _resident((1, d)),
                  _resident((d, p))],
        out_specs=[pl.BlockSpec((tm, w), lambda i: (i, 0)) for _, w in _PROJ_GROUPS],
        compiler_params=_params(1),
        name="inproj",
    )(x2d, mod3, mod3, norm_w, w_perm)


def _tri(rev):
    row = lax.broadcasted_iota(jnp.int32, (CHUNK, CHUNK), 0)
    col = lax.broadcasted_iota(jnp.int32, (CHUNK, CHUNK), 1)
    return (col >= row) if rev else (col <= row)


def _gla_chunk(d, r0, qk_ref, v_ref, sm_ref, lrw_ref, lrb_ref, o_ref, s_ref):
    rev = d == 1
    incl = _tri(rev)
    rows = pl.ds(r0, CHUNK)
    qk = qk_ref[rows, :]
    v = v_ref[rows, :]
    gpre = _dot(sm_ref[rows, :], lrw_ref[d], precision=HI) + lrb_ref[d]
    g = (jnp.minimum(gpre, 0.0) - jnp.log(1.0 + jnp.exp(-jnp.abs(gpre)))) * (1.0 / GLA_GATE_NORM)
    G = _dot(incl.astype(F32), g, precision=HI)
    g_mid = G[CHUNK // 2:CHUNK // 2 + 1, :]
    g_end = G[0:1, :] if rev else G[CHUNK - 1:CHUNK, :]
    e_g = jnp.exp(G)
    e_q = jnp.exp(G - g_mid)
    e_k = jnp.exp(g_mid - G)
    e_kend = jnp.exp(g_end - G)
    e_end = jnp.exp(g_end)
    nqk = GLA_HEADS * GLA_DK
    for h in range(GLA_HEADS):
        sl = slice(h * GLA_DK, (h + 1) * GLA_DK)
        q = qk[:, sl] * (GLA_DK ** -0.5)
        k = qk[:, nqk + h * GLA_DK:nqk + (h + 1) * GLA_DK]
        vh = v[:, h * GLA_DV:(h + 1) * GLA_DV].astype(BF16)
        a = _dot_nt((q * e_q[:, sl]).astype(BF16), (k * e_k[:, sl]).astype(BF16))
        a = jnp.where(incl, a, 0.0)
        s = s_ref[d, h]
        o = _dot(a.astype(BF16), vh) + _dot((q * e_g[:, sl]).astype(BF16), s.astype(BF16))
        o_ref[rows, h * GLA_DV:(h + 1) * GLA_DV] = o
        decay = jnp.broadcast_to(e_end[:, sl], (GLA_DK, GLA_DK)).T
        decay = jnp.concatenate([decay] * (GLA_DV // GLA_DK), axis=1)
        s_ref[d, h] = decay * s + _dot_tn((k * e_kend[:, sl]).astype(BF16), vh)


def _gla_kernel(qkf, vf, smf, qkb, vb, smb, lrw, lrb, s0, of, ob, sfin, s_ref, *, nchunk):
    i = pl.program_id(1)

    @pl.when(i == 0)
    def _():
        s_ref[...] = s0[...]

    def body(c, carry):
        _gla_chunk(0, pl.multiple_of(c * CHUNK, CHUNK), qkf, vf, smf, lrw, lrb, of, s_ref)
        _gla_chunk(1, pl.multiple_of((nchunk - 1 - c) * CHUNK, CHUNK), qkb, vb, smb, lrw, lrb, ob, s_ref)
        return carry

    lax.fori_loop(0, nchunk, body, 0)

    @pl.when(i == pl.num_programs(1) - 1)
    def _():
        sfin[...] = s_ref[...]


def _gla_call(gqk, gv, sm, lrw_pad, lrb, s0, tb):
    b, l, _ = gqk.shape
    nblk = l // tb
    fwd = lambda bi, i: (bi, i, 0)
    bwd = lambda bi, i: (bi, nblk - 1 - i, 0)
    sspec = pl.BlockSpec((None, 2, GLA_HEADS, GLA_DK, GLA_DV), lambda bi, i: (bi, 0, 0, 0, 0))
    wide = GLA_HEADS * GLA_DV
    return pl.pallas_call(
        functools.partial(_gla_kernel, nchunk=tb // CHUNK),
        out_shape=[jax.ShapeDtypeStruct((b, l, wide), F32),
                   jax.ShapeDtypeStruct((b, l, wide), F32),
                   jax.ShapeDtypeStruct(s0.shape, F32)],
        grid=(b, nblk),
        in_specs=[pl.BlockSpec((None, tb, 2 * GLA_HEADS * GLA_DK), fwd),
                  pl.BlockSpec((None, tb, wide), fwd),
                  pl.BlockSpec((None, tb, LANES), fwd),
                  pl.BlockSpec((None, tb, 2 * GLA_HEADS * GLA_DK), bwd),
                  pl.BlockSpec((None, tb, wide), bwd),
                  pl.BlockSpec((None, tb, LANES), bwd),
                  _resident(lrw_pad.shape),
                  _resident(lrb.shape),
                  sspec],
        out_specs=[pl.BlockSpec((None, tb, wide), fwd),
                   pl.BlockSpec((None, tb, wide), bwd),
                   sspec],
        scratch_shapes=[pltpu.VMEM((2, GLA_HEADS, GLA_DK, GLA_DV), F32)],
        compiler_params=_params(2),
        name="gla",
    )(gqk, gv, sm, gqk, gv, sm, lrw_pad, lrb, s0)


def _gdn_prepare(d, x_ref, sm_ref, convw_ref, alog_ref, dtb_ref, expand_ref,
                 xp_ref, q_ref, k_ref, v_ref, gx_ref, bx_ref, *, rows):
    rev = d == 1
    width = 3 * GDN_HEADS * GDN_D
    zeros = jnp.zeros((CONV_PAD, width), F32)
    xp_ref[0:CONV_PAD, :] = zeros
    xp_ref[CONV_PAD + rows:CONV_PAD + rows + CONV_PAD, :] = zeros
    xp_ref[CONV_PAD:CONV_PAD + rows, :] = x_ref[...]
    acc = None
    for i in range(CONV_W):
        start = CONV_PAD + i - CONV_W // 2
        term = convw_ref[i:i + 1, :] * xp_ref[start:start + rows, :]
        acc = term if acc is None else acc + term
    xs = _silu(acc)
    nq = GDN_HEADS * GDN_D
    for h in range(GDN_HEADS):
        qh = xs[:, h * GDN_D:(h + 1) * GDN_D]
        kh = xs[:, nq + h * GDN_D:nq + (h + 1) * GDN_D]
        q_ref[d, h] = qh * (lax.rsqrt(jnp.sum(qh * qh, axis=-1, keepdims=True) + EPS) * GDN_D ** -0.5)
        k_ref[d, h] = kh * lax.rsqrt(jnp.sum(kh * kh, axis=-1, keepdims=True) + EPS)
        v_ref[d, h] = xs[:, 2 * nq + h * GDN_D:2 * nq + (h + 1) * GDN_D]
    sm = sm_ref[...]
    gate = -jnp.exp(alog_ref[...]) * _softplus(sm + dtb_ref[...])
    tri = _tri(rev).astype(F32)
    gcum = jnp.concatenate(
        [_dot(tri, gate[c * CHUNK:(c + 1) * CHUNK, :], precision=HI) for c in range(rows // CHUNK)], axis=0)
    gx = _dot(gcum, expand_ref[2 * d], precision=HI)
    bx = _dot(_sigmoid(sm), expand_ref[2 * d + 1], precision=HI)
    for h in range(GDN_HEADS):
        gx_ref[d, h] = gx[:, h * GDN_D:(h + 1) * GDN_D]
        bx_ref[d, h] = bx[:, h * GDN_D:(h + 1) * GDN_D]


def _gdn_chunk(d, h, c, q_ref, k_ref, v_ref, gx_ref, bx_ref, o_ref, s_ref):
    rev = d == 1
    rows = slice(c * CHUNK, (c + 1) * CHUNK)
    row = lax.broadcasted_iota(jnp.int32, (CHUNK, CHUNK), 0)
    col = lax.broadcasted_iota(jnp.int32, (CHUNK, CHUNK), 1)
    eye = row == col
    strict = (col > row) if rev else (col < row)
    eye_f = eye.astype(F32)
    q = q_ref[d, h, rows, :]
    k = k_ref[d, h, rows, :]
    v = v_ref[d, h, rows, :]
    gc = gx_ref[d, h, rows, :]
    beta = bx_ref[d, h, rows, :]
    g_end = gc[0:1, :] if rev else gc[CHUNK - 1:CHUNK, :]
    e_g = jnp.exp(gc)
    gi = gc[:, :CHUNK]
    gj = _dot(jnp.ones((CHUNK, CHUNK), F32), jnp.where(eye, gi, 0.0), precision=HI)
    dec = jnp.exp(jnp.where(strict, gi - gj, -1e30))
    kb = k.astype(BF16)
    qkk = _dot_nt(jnp.concatenate([q, k], axis=0).astype(BF16), kb)
    x = -(beta[:, :CHUNK] * qkk[CHUNK:] * dec)
    p = eye_f + jnp.where((row >> 1) == (col >> 1), x, 0.0)
    for level in range(1, 6):
        sibling = ((row >> level) ^ 1) == (col >> level)
        xs = jnp.where(sibling, x, 0.0).astype(BF16)
        pb = p.astype(BF16)
        p = p + _dot(pb, _dot(xs, pb).astype(BF16))
    rhs = jnp.concatenate([beta * v, beta * e_g * k], axis=1)
    uw = _dot(p.astype(BF16), rhs.astype(BF16))
    u, w = uw[:, :GDN_D], uw[:, GDN_D:]
    a_qk = qkk[:CHUNK] * (dec + eye_f)
    s = s_ref[d, h]
    wq = _dot(jnp.concatenate([w, q * e_g], axis=0).astype(BF16), s.astype(BF16))
    v_new = u - wq[:CHUNK]
    v_new_b = v_new.astype(BF16)
    o_ref[d, h, rows, :] = wq[CHUNK:] + _dot(a_qk.astype(BF16), v_new_b)
    s_ref[d, h] = jnp.exp(g_end) * s + _dot_tn((k * jnp.exp(g_end - gc)).astype(BF16), v_new_b)


def _gdn_kernel(xf, smf, xb, smb, convw, alog, dtb, expand, s0, of, ob, sfin,
                s_ref, xp_ref, q_ref, k_ref, v_ref, gx_ref, bx_ref, o_ref, *, rows):
    i = pl.program_id(1)

    @pl.when(i == 0)
    def _():
        s_ref[...] = s0[...]

    scratch = (xp_ref, q_ref, k_ref, v_ref, gx_ref, bx_ref)
    _gdn_prepare(0, xf, smf, convw, alog, dtb, expand, *scratch, rows=rows)
    _gdn_prepare(1, xb, smb, convw, alog, dtb, expand, *scratch, rows=rows)
    nchunk = rows // CHUNK

    def head(h, carry):
        for c in range(nchunk):
            _gdn_chunk(0, h, c, q_ref, k_ref, v_ref, gx_ref, bx_ref, o_ref, s_ref)
            _gdn_chunk(1, h, nchunk - 1 - c, q_ref, k_ref, v_ref, gx_ref, bx_ref, o_ref, s_ref)
        return carry

    lax.fori_loop(0, GDN_HEADS, head, 0)
    for h in range(GDN_HEADS):
        of[:, h * GDN_D:(h + 1) * GDN_D] = o_ref[0, h]
        ob[:, h * GDN_D:(h + 1) * GDN_D] = o_ref[1, h]

    @pl.when(i == pl.num_programs(1) - 1)
    def _():
        sfin[...] = s_ref[...]


def _gdn_call(dqkv, sm, conv_w, alog_row, dtb_row, expand, s0, rows):
    b = dqkv.shape[0]
    width = 3 * GDN_HEADS * GDN_D
    wide = GDN_HEADS * GDN_D
    ncol = dqkv.shape[2] // width
    fwd = lambda bi, i: (bi, 0, i)
    bwd = lambda bi, i: (bi, 0, ncol - 1 - i)
    sspec = pl.BlockSpec((None, 2, GDN_HEADS, GDN_D, GDN_D), lambda bi, i: (bi, 0, 0, 0, 0))
    per_head = pltpu.VMEM((2, GDN_HEADS, rows, GDN_D), F32)
    return pl.pallas_call(
        functools.partial(_gdn_kernel, rows=rows),
        out_shape=[jax.ShapeDtypeStruct((b, rows, ncol * wide), F32),
                   jax.ShapeDtypeStruct((b, rows, ncol * wide), F32),
                   jax.ShapeDtypeStruct(s0.shape, F32)],
        grid=(b, ncol),
        in_specs=[pl.BlockSpec((None, rows, width), fwd),
                  pl.BlockSpec((None, rows, LANES), fwd),
                  pl.BlockSpec((None, rows, width), bwd),
                  pl.BlockSpec((None, rows, LANES), bwd),
                  _resident(conv_w.shape),
                  _resident(alog_row.shape),
                  _resident(dtb_row.shape),
                  _resident(expand.shape),
                  sspec],
        out_specs=[pl.BlockSpec((None, rows, wide), fwd),
                   pl.BlockSpec((None, rows, wide), bwd),
                   sspec],
        scratch_shapes=[pltpu.VMEM((2, GDN_HEADS, GDN_D, GDN_D), F32),
                        pltpu.VMEM((rows + 2 * CONV_PAD, width), F32),
                        per_head, per_head, per_head, per_head, per_head, per_head],
        compiler_params=_params(2),
        name="gdn",
    )(dqkv, sm, dqkv, sm, conv_w, alog_row, dtb_row, expand, s0)


def _head_norm(o, heads, gain):
    dv = o.shape[-1] // heads
    parts = []
    for h in range(heads):
        oh = o[:, h * dv:(h + 1) * dv]
        parts.append(oh * lax.rsqrt(jnp.mean(oh * oh, axis=-1, keepdims=True) + EPS))
    return jnp.concatenate(parts, axis=-1) * gain


def _tail_kernel(x_ref, glaf, glab, gz, gdnf, gdnb, dz, og, gate1, shift2, scale2, gate2,
                 gla_gain, gdn_gain, n2w, fw, w_out, w_gate, w_up, w_down, out_ref):
    d = x_ref.shape[-1]
    y_gla = _head_norm(glaf[...] + glab[...], GLA_HEADS, gla_gain[...]) * _silu(gz[...])
    y_gdn = _head_norm(gdnf[...] + gdnb[...], GDN_HEADS, gdn_gain[...]) * _silu(dz[...])
    merged = _sigmoid(og[:, :d]) * y_gla + _sigmoid(og[:, d:]) * y_gdn
    x1 = x_ref[...] + gate1[...] * _dot(merged.astype(BF16), w_out[...])
    h2 = (_rms(x1, n2w[...]) * (1.0 + scale2[...]) + shift2[...]).astype(BF16)
    hidden = _silu(_dot(h2, w_gate[...])) * _dot(h2, w_up[...])
    x2 = x1 + gate2[...] * _dot(hidden.astype(BF16), w_down[...])
    out_ref[...] = _rms(x2, fw[...])


def _tail_call(x2d, glaf, glab, gz, gdnf, gdnb, dz, og, mod3, gla_gain, gdn_gain, n2w, fw,
               w_out, w_gate, w_up, w_down, tokens_per_mod, tm):
    n, d = x2d.shape
    mod_row = lambda i: (i * tm) // tokens_per_mod
    tok = lambda w: pl.BlockSpec((tm, w), lambda i: (i, 0))
    modspec = lambda j: pl.BlockSpec((None, 1, d), lambda i: (mod_row(i), 0, j))
    return pl.pallas_call(
        _tail_kernel,
        out_shape=jax.ShapeDtypeStruct((n, d), F32),
        grid=(n // tm,),
        in_specs=[tok(d)] * 7 + [tok(2 * d)]
                 + [modspec(2), modspec(3), modspec(4), modspec(5)]
                 + [_resident(a.shape) for a in (gla_gain, gdn_gain, n2w, fw, w_out, w_gate, w_up, w_down)],
        out_specs=tok(d),
        compiler_params=_params(1),
        name="tail",
    )(x2d, glaf, glab, gz, gdnf, gdnb, dz, og, mod3, mod3, mod3, mod3,
      gla_gain, gdn_gain, n2w, fw, w_out, w_gate, w_up, w_down)


def _permute_w_in(w_in):
    d = w_in.shape[0]
    gq, gk, gv, gz, lrf, lrb, dq, dk, dv, dz, ab, p11, p12 = jnp.split(
        w_in, np.cumsum([512, 512, 1024, 1024, 16, 16, 1024, 1024, 1024, 1024, 32, 1024])[:].tolist(), axis=1)
    pad = jnp.zeros((d, LANES - 2 * GLA_LOWRANK - 4 * GDN_HEADS), w_in.dtype)
    return jnp.concatenate([gq, gk, gv, gz, dq, dk, dv, dz, p11, p12, lrf, lrb, ab, pad], axis=1)


def _gdn_expand():
    e = np.zeros((4, LANES, GDN_HEADS * GDN_D), np.float32)
    for d in range(2):
        for h in range(GDN_HEADS):
            e[2 * d, SM_A + GDN_HEADS * d + h, h * GDN_D:(h + 1) * GDN_D] = 1.0
            e[2 * d + 1, SM_B + GDN_HEADS * d + h, h * GDN_D:(h + 1) * GDN_D] = 1.0
    return jnp.asarray(e)


def kernel(x, c, ctx, c_ctx, w_mod, b_mod, norm1_w, norm2_w, w_in, gla_lr_w, gla_lr_b, gla_norm_w, gdn_conv_w, gdn_a_log, gdn_dt_bias, gdn_norm_w, w_out, ffn_w_gate, ffn_w_up, ffn_w_down, final_norm_w):
    b, seq, d = x.shape
    ctx_len = ctx.shape[1]
    depth = w_mod.shape[0]
    assert depth == 1, "only the single-layer block is implemented"
    assert d == GLA_HEADS * GLA_DV == GDN_HEADS * GDN_D
    rows = seq // GRID_W
    assert seq % GRID_W == 0 and rows % CHUNK == 0 and ctx_len % CHUNK == 0
    layer = 0

    mod_rows = -(-(b + 1) // 8) * 8
    cc = jnp.zeros((mod_rows, d), F32).at[:b].set(c).at[b].set(c_ctx)
    mod = _mod_call(cc, w_mod[layer], b_mod[layer][None, :])
    mod_l = mod[:b, None, :]
    mod_c = mod[b:b + 1, None, :]

    w_perm = _permute_w_in(w_in[layer]).astype(BF16)
    n1w = norm1_w[layer][None, :]
    tm = 256
    proj_l = _inproj_call(x.reshape(b * seq, d), mod_l, n1w, w_perm, seq, tm)
    proj_c = _inproj_call(ctx.reshape(b * ctx_len, d), mod_c, n1w, w_perm, b * ctx_len, tm)
    gqk_l, gv_l, gz_l, dqkv_l, dz_l, og_l, sm_l = proj_l
    gqk_c, gv_c, _, dqkv_c, _, _, sm_c = proj_c

    lrw_pad = jnp.zeros((2, LANES, GLA_HEADS * GLA_DK), F32)
    lrw_pad = lrw_pad.at[0, SM_LRF:SM_LRF + GLA_LOWRANK].set(gla_lr_w[layer, 0])
    lrw_pad = lrw_pad.at[1, SM_LRB:SM_LRB + GLA_LOWRANK].set(gla_lr_w[layer, 1])
    lrb = gla_lr_b[layer][:, None, :]
    r3 = lambda t, l: t.reshape(b, l, t.shape[-1])
    s0_gla = jnp.zeros((b, 2, GLA_HEADS, GLA_DK, GLA_DV), F32)
    _, _, s_gla = _gla_call(r3(gqk_c, ctx_len), r3(gv_c, ctx_len), r3(sm_c, ctx_len), lrw_pad, lrb, s0_gla, ctx_len)
    gla_f, gla_b, _ = _gla_call(r3(gqk_l, seq), r3(gv_l, seq), r3(sm_l, seq), lrw_pad, lrb, s_gla, 256)

    lane_params = lambda p: jnp.zeros((1, LANES), F32).at[0, SM_A:SM_A + 2 * GDN_HEADS].set(p[layer].reshape(-1))
    alog_row, dtb_row = lane_params(gdn_a_log), lane_params(gdn_dt_bias)
    expand = _gdn_expand()
    conv_w = gdn_conv_w[layer]
    s0_gdn = jnp.zeros((b, 2, GDN_HEADS, GDN_D, GDN_D), F32)
    _, _, s_gdn = _gdn_call(r3(dqkv_c, ctx_len), r3(sm_c, ctx_len), conv_w, alog_row, dtb_row, expand, s0_gdn, ctx_len)
    col = lambda t: t.reshape(b, rows, GRID_W * t.shape[-1])
    gdn_f, gdn_b, _ = _gdn_call(col(dqkv_l), col(sm_l), conv_w, alog_row, dtb_row, expand, s_gdn, rows)

    flat = lambda t: t.reshape(b * seq, d)
    out = _tail_call(
        x.reshape(b * seq, d), flat(gla_f), flat(gla_b), gz_l, flat(gdn_f), flat(gdn_b), dz_l, og_l, mod_l,
        gla_norm_w[layer].reshape(1, d), gdn_norm_w[layer].reshape(1, d),
        norm2_w[layer][None, :], final_norm_w[None, :],
        w_out[layer].astype(BF16), ffn_w_gate[layer].astype(BF16), ffn_w_up[layer].astype(BF16),
        ffn_w_down[layer].astype(BF16), seq, tm)
    return out.reshape(b, seq, d)
```

```python
import functools

import numpy as np
import jax
import jax.numpy as jnp
from jax import lax
from jax.experimental import pallas as pl
from jax.experimental.pallas import tpu as pltpu

F32 = jnp.float32
BF16 = jnp.bfloat16
HI = lax.Precision.HIGHEST

EPS = 1e-6
CHUNK = 64
GRID_W = 64
CONV_W = 5
CONV_PAD = 8

GLA_HEADS = 4
GLA_DK = 128
GLA_DV = 256
GLA_LOWRANK = 16
GLA_GATE_NORM = 16.0
GDN_HEADS = 8
GDN_D = 128

LANES = 128
VMEM_LIMIT = 56 * 1024 * 1024

SM_LRF, SM_LRB, SM_A, SM_B = 0, 16, 32, 48


def _dot(a, b, **kw):
    return jnp.dot(a, b, preferred_element_type=F32, **kw)


def _dot_nt(a, b):
    return lax.dot_general(a, b, (((1,), (1,)), ((), ())), preferred_element_type=F32)


def _dot_tn(a, b):
    return lax.dot_general(a, b, (((0,), (0,)), ((), ())), preferred_element_type=F32)


def _sigmoid(t):
    return 1.0 / (1.0 + jnp.exp(-t))


def _silu(t):
    return t * _sigmoid(t)


def _softplus(t):
    return jnp.maximum(t, 0.0) + jnp.log(1.0 + jnp.exp(-jnp.abs(t)))


def _rms(t, w):
    return t * lax.rsqrt(jnp.mean(t * t, axis=-1, keepdims=True) + EPS) * w


def _params(n_grid):
    return pltpu.CompilerParams(dimension_semantics=("arbitrary",) * n_grid,
                                vmem_limit_bytes=VMEM_LIMIT)


def _resident(shape):
    return pl.BlockSpec(shape, lambda *_: (0,) * len(shape), pipeline_mode=pl.Buffered(1))


def _mod_kernel(c_ref, w_ref, b_ref, o_ref):
    o_ref[...] = _dot(_silu(c_ref[...]), w_ref[...], precision=HI) + b_ref[...]


def _mod_call(cc, w_mod, b_mod):
    rows, d = cc.shape
    n = w_mod.shape[1]
    return pl.pallas_call(
        _mod_kernel,
        out_shape=jax.ShapeDtypeStruct((rows, n), F32),
        grid=(n // d,),
        in_specs=[pl.BlockSpec((rows, d), lambda j: (0, 0)),
                  pl.BlockSpec((d, d), lambda j: (0, j)),
                  pl.BlockSpec((1, d), lambda j: (0, j))],
        out_specs=pl.BlockSpec((rows, d), lambda j: (0, j)),
        compiler_params=_params(1),
        name="mod",
    )(cc, w_mod, b_mod)


_PROJ_GROUPS = (("gqk", 1024), ("gv", 1024), ("gz", 1024), ("dqkv", 3072), ("dz", 1024),
                ("og", 2048), ("sm", LANES))


def _inproj_kernel(x_ref, shift_ref, scale_ref, nw_ref, w_ref, *out_refs):
    h = _rms(x_ref[...], nw_ref[...]) * (1.0 + scale_ref[...]) + shift_ref[...]
    hb = h.astype(BF16)
    lo = 0
    for ref, (_, width) in zip(out_refs, _PROJ_GROUPS):
        ref[...] = _dot(hb, w_ref[:, lo:lo + width])
        lo += width


def _inproj_call(x2d, mod3, norm_w, w_perm, tokens_per_mod, tm):
    n, d = x2d.shape
    p = w_perm.shape[1]
    mod_row = lambda i: (i * tm) // tokens_per_mod
    return pl.pallas_call(
        _inproj_kernel,
        out_shape=[jax.ShapeDtypeStruct((n, w), F32) for _, w in _PROJ_GROUPS],
        grid=(n // tm,),
        in_specs=[pl.BlockSpec((tm, d), lambda i: (i, 0)),
                  pl.BlockSpec((None, 1, d), lambda i: (mod_row(i), 0, 0)),
                  pl.BlockSpec((None, 1, d), lambda i: (mod_row(i), 0, 1)),
                  _resident((1, d)),
                  _resident((d, p))],
        out_specs=[pl.BlockSpec((tm, w), lambda i: (i, 0)) for _, w in _PROJ_GROUPS],
        compiler_params=_params(1),
        name="inproj",
    )(x2d, mod3, mod3, norm_w, w_perm)


def _tri(rev):
    row = lax.broadcasted_iota(jnp.int32, (CHUNK, CHUNK), 0)
    col = lax.broadcasted_iota(jnp.int32, (CHUNK, CHUNK), 1)
    return (col >= row) if rev else (col <= row)


def _gla_chunk(d, r0, qk_ref, v_ref, sm_ref, lrw_ref, lrb_ref, o_ref, s_ref):
    rev = d == 1
    incl = _tri(rev)
    rows = pl.ds(r0, CHUNK)
    qk = qk_ref[rows, :]
    v = v_ref[rows, :]
    gpre = _dot(sm_ref[rows, :], lrw_ref[d], precision=HI) + lrb_ref[d]
    g = (jnp.minimum(gpre, 0.0) - jnp.log(1.0 + jnp.exp(-jnp.abs(gpre)))) * (1.0 / GLA_GATE_NORM)
    G = _dot(incl.astype(F32), g, precision=HI)
    g_mid = G[CHUNK // 2:CHUNK // 2 + 1, :]
    g_end = G[0:1, :] if rev else G[CHUNK - 1:CHUNK, :]
    e_g = jnp.exp(G)
    e_q = jnp.exp(G - g_mid)
    e_k = jnp.exp(g_mid - G)
    e_kend = jnp.exp(g_end - G)
    e_end = jnp.exp(g_end)
    nqk = GLA_HEADS * GLA_DK
    for h in range(GLA_HEADS):
        sl = slice(h * GLA_DK, (h + 1) * GLA_DK)
        q = qk[:, sl] * (GLA_DK ** -0.5)
        k = qk[:, nqk + h * GLA_DK:nqk + (h + 1) * GLA_DK]
        vh = v[:, h * GLA_DV:(h + 1) * GLA_DV].astype(BF16)
        a = _dot_nt((q * e_q[:, sl]).astype(BF16), (k * e_k[:, sl]).astype(BF16))
        a = jnp.where(incl, a, 0.0)
        s = s_ref[d, h]
        o = _dot(a.astype(BF16), vh) + _dot((q * e_g[:, sl]).astype(BF16), s.astype(BF16))
        o_ref[rows, h * GLA_DV:(h + 1) * GLA_DV] = o
        decay = jnp.broadcast_to(e_end[:, sl], (GLA_DK, GLA_DK)).T
        decay = jnp.concatenate([decay] * (GLA_DV // GLA_DK), axis=1)
        s_ref[d, h] = decay * s + _dot_tn((k * e_kend[:, sl]).astype(BF16), vh)


def _gla_kernel(qkf, vf, smf, qkb, vb, smb, lrw, lrb, s0, of, ob, sfin, s_ref, *, nchunk):
    i = pl.program_id(1)

    @pl.when(i == 0)
    def _():
        s_ref[...] = s0[...]

    def body(c, carry):
        _gla_chunk(0, pl.multiple_of(c * CHUNK, CHUNK), qkf, vf, smf, lrw, lrb, of, s_ref)
        _gla_chunk(1, pl.multiple_of((nchunk - 1 - c) * CHUNK, CHUNK), qkb, vb, smb, lrw, lrb, ob, s_ref)
        return carry

    lax.fori_loop(0, nchunk, body, 0)

    @pl.when(i == pl.num_programs(1) - 1)
    def _():
        sfin[...] = s_ref[...]


def _gla_call(gqk, gv, sm, lrw_pad, lrb, s0, tb):
    b, l, _ = gqk.shape
    nblk = l // tb
    fwd = lambda bi, i: (bi, i, 0)
    bwd = lambda bi, i: (bi, nblk - 1 - i, 0)
    sspec = pl.BlockSpec((None, 2, GLA_HEADS, GLA_DK, GLA_DV), lambda bi, i: (bi, 0, 0, 0, 0))
    wide = GLA_HEADS * GLA_DV
    return pl.pallas_call(
        functools.partial(_gla_kernel, nchunk=tb // CHUNK),
        out_shape=[jax.ShapeDtypeStruct((b, l, wide), F32),
                   jax.ShapeDtypeStruct((b, l, wide), F32),
                   jax.ShapeDtypeStruct(s0.shape, F32)],
        grid=(b, nblk),
        in_specs=[pl.BlockSpec((None, tb, 2 * GLA_HEADS * GLA_DK), fwd),
                  pl.BlockSpec((None, tb, wide), fwd),
                  pl.BlockSpec((None, tb, LANES), fwd),
                  pl.BlockSpec((None, tb, 2 * GLA_HEADS * GLA_DK), bwd),
                  pl.BlockSpec((None, tb, wide), bwd),
                  pl.BlockSpec((None, tb, LANES), bwd),
                  _resident(lrw_pad.shape),
                  _resident(lrb.shape),
                  sspec],
        out_specs=[pl.BlockSpec((None, tb, wide), fwd),
                   pl.BlockSpec((None, tb, wide), bwd),
                   sspec],
        scratch_shapes=[pltpu.VMEM((2, GLA_HEADS, GLA_DK, GLA_DV), F32)],
        compiler_params=_params(2),
        name="gla",
    )(gqk, gv, sm, gqk, gv, sm, lrw_pad, lrb, s0)


def _split3(t):
    hi = t.astype(BF16)
    r1 = t - hi.astype(F32)
    mid = r1.astype(BF16)
    lo = (r1 - mid.astype(F32)).astype(BF16)
    return hi, mid, lo


def _gdn_prepare(d, x_ref, sm_ref, convw_ref, alog_ref, dtb_ref, expand_ref,
                 xp_ref, qkv_ref, gx_ref, gt_ref, bt_ref, *, rows):
    rev = d == 1
    nq = GDN_HEADS * GDN_D
    width = 3 * nq
    zeros = jnp.zeros((CONV_PAD, width), F32)
    xp_ref[0:CONV_PAD, :] = zeros
    xp_ref[CONV_PAD + rows:CONV_PAD + rows + CONV_PAD, :] = zeros
    xp_ref[CONV_PAD:CONV_PAD + rows, :] = x_ref[...]
    acc = None
    for i in range(CONV_W):
        start = CONV_PAD + i - CONV_W // 2
        term = convw_ref[i:i + 1, :] * xp_ref[start:start + rows, :]
        acc = term if acc is None else acc + term
    xs = _silu(acc)
    for h in range(GDN_HEADS):
        qh = xs[:, h * GDN_D:(h + 1) * GDN_D]
        kh = xs[:, nq + h * GDN_D:nq + (h + 1) * GDN_D]
        qn = qh * (lax.rsqrt(jnp.sum(qh * qh, axis=-1, keepdims=True) + EPS) * GDN_D ** -0.5)
        kn = kh * lax.rsqrt(jnp.sum(kh * kh, axis=-1, keepdims=True) + EPS)
        qkv_ref[d, :, h * GDN_D:(h + 1) * GDN_D] = qn.astype(BF16)
        qkv_ref[d, :, nq + h * GDN_D:nq + (h + 1) * GDN_D] = kn.astype(BF16)
    qkv_ref[d, :, 2 * nq:] = xs[:, 2 * nq:].astype(BF16)
    sm = sm_ref[...]
    gate = -jnp.exp(alog_ref[...]) * _softplus(sm + dtb_ref[...])
    tri = _tri(rev).astype(F32)
    nchunk = rows // CHUNK
    gcum = jnp.concatenate(
        [_dot(tri, gate[c * CHUNK:(c + 1) * CHUNK, :], precision=HI) for c in range(nchunk)], axis=0)
    onehot = expand_ref[d]
    hi, mid, lo = _split3(gcum)
    gx_ref[d] = _dot(hi, onehot) + _dot(mid, onehot) + _dot(lo, onehot)
    g_t = gcum.T
    b_t = _sigmoid(sm).T
    for c in range(nchunk):
        gt_ref[d, c] = g_t[:, c * CHUNK:(c + 1) * CHUNK]
        bt_ref[d, c] = b_t[:, c * CHUNK:(c + 1) * CHUNK]


def _gdn_local(chains, qkv_ref, gx_ref, gt_ref, bt_ref):
    nq = GDN_HEADS * GDN_D
    row = lax.broadcasted_iota(jnp.int32, (CHUNK, CHUNK), 0)
    col = lax.broadcasted_iota(jnp.int32, (CHUNK, CHUNK), 1)
    eye = row == col
    eye_f = eye.astype(F32)
    st = []
    for d, h, c in chains:
        rev = d == 1
        rs = slice(c * CHUNK, (c + 1) * CHUNK)
        lane_g = SM_A + GDN_HEADS * d + h
        lane_b = SM_B + GDN_HEADS * d + h
        end = c * CHUNK if rev else (c + 1) * CHUNK - 1
        q = qkv_ref[d, rs, h * GDN_D:(h + 1) * GDN_D]
        k = qkv_ref[d, rs, nq + h * GDN_D:nq + (h + 1) * GDN_D]
        st.append(dict(
            d=d, h=h, c=c, rs=rs, k=k,
            strict=(col > row) if rev else (col < row),
            qk=jnp.concatenate([q, k], axis=0),
            v=qkv_ref[d, rs, 2 * nq + h * GDN_D:2 * nq + (h + 1) * GDN_D],
            gj=gt_ref[d, c, lane_g:lane_g + 1, :],
            bj=bt_ref[d, c, lane_b:lane_b + 1, :],
            gi=gx_ref[d, rs, h * GDN_D:h * GDN_D + CHUNK],
            g_end=gx_ref[d, end:end + 1, h * GDN_D:(h + 1) * GDN_D]))
    for s in st:
        s["qkk"] = _dot_nt(s["qk"], s["k"])
    for s in st:
        dec = jnp.exp(jnp.where(s["strict"], s["gi"] - s["gj"], -1e30))
        s["dec"] = dec
        x = -(s["qkk"][CHUNK:] * dec * s["bj"])
        s["x"] = x
        s["p"] = eye_f + jnp.where((row >> 1) == (col >> 1), x, 0.0)
    for level in range(1, 6):
        sibling = ((row >> level) ^ 1) == (col >> level)
        for s in st:
            s["pb"] = s["p"].astype(BF16)
            s["t"] = _dot(jnp.where(sibling, s["x"], 0.0).astype(BF16), s["pb"])
        for s in st:
            s["p"] = s["p"] + _dot(s["pb"], s["t"].astype(BF16))
    for s in st:
        rb = s["p"].astype(BF16)
        s["rv"] = _dot(rb, s["v"])
    for s in st:
        e_gj = jnp.exp(s["gj"])
        s["rg"] = (s["p"] * e_gj).astype(BF16)
        s["e"] = jnp.where(eye, e_gj, 0.0).astype(BF16)
        a_beta = s["qkk"][:CHUNK] * (s["dec"] + eye_f) * s["bj"]
        kscale = jnp.where(eye, jnp.exp(s["g_end"][:, :CHUNK] - s["gj"]) * s["bj"], 0.0)
        s["ak"] = jnp.concatenate([a_beta, kscale], axis=0).astype(BF16)
    return st


def _gdn_recur(st, o_refs, s_ref):
    for s in st:
        s["s"] = s_ref[s["d"], s["h"]]
        s["qks"] = _dot(s["qk"], s["s"].astype(BF16))
    for s in st:
        s["vhat"] = (s["rv"] - _dot(s["rg"], s["qks"][CHUNK:].astype(BF16))).astype(BF16)
    for s in st:
        s["av"] = _dot(s["ak"], s["vhat"])
        s["eq"] = _dot(s["e"], s["qks"][:CHUNK].astype(BF16))
    for s in st:
        h = s["h"]
        o_refs[s["d"]][s["rs"], h * GDN_D:(h + 1) * GDN_D] = s["eq"] + s["av"][:CHUNK]
        s_ref[s["d"], h] = jnp.exp(s["g_end"]) * s["s"] + _dot_tn(s["k"], s["av"][CHUNK:].astype(BF16))


def _gdn_kernel(xf, smf, xb, smb, convw, alog, dtb, expand, s0, of, ob, sfin,
                s_ref, xp_ref, qkv_ref, gx_ref, gt_ref, bt_ref, *, rows):
    i = pl.program_id(1)

    @pl.when(i == 0)
    def _():
        s_ref[...] = s0[...]

    scratch = (xp_ref, qkv_ref, gx_ref, gt_ref, bt_ref)
    _gdn_prepare(0, xf, smf, convw, alog, dtb, expand, *scratch, rows=rows)
    _gdn_prepare(1, xb, smb, convw, alog, dtb, expand, *scratch, rows=rows)
    nchunk = rows // CHUNK
    for step in range(nchunk):
        chains = [(0, h, step) for h in range(GDN_HEADS)] + [(1, h, nchunk - 1 - step) for h in range(GDN_HEADS)]
        _gdn_recur(_gdn_local(chains, qkv_ref, gx_ref, gt_ref, bt_ref), (of, ob), s_ref)

    @pl.when(i == pl.num_programs(1) - 1)
    def _():
        sfin[...] = s_ref[...]


def _gdn_call(dqkv, sm, conv_w, alog_row, dtb_row, expand, s0, rows):
    b = dqkv.shape[0]
    width = 3 * GDN_HEADS * GDN_D
    wide = GDN_HEADS * GDN_D
    ncol = dqkv.shape[2] // width
    nchunk = rows // CHUNK
    fwd = lambda bi, i: (bi, 0, i)
    bwd = lambda bi, i: (bi, 0, ncol - 1 - i)
    sspec = pl.BlockSpec((None, 2, GDN_HEADS, GDN_D, GDN_D), lambda bi, i: (bi, 0, 0, 0, 0))
    return pl.pallas_call(
        functools.partial(_gdn_kernel, rows=rows),
        out_shape=[jax.ShapeDtypeStruct((b, rows, ncol * wide), F32),
                   jax.ShapeDtypeStruct((b, rows, ncol * wide), F32),
                   jax.ShapeDtypeStruct(s0.shape, F32)],
        grid=(b, ncol),
        in_specs=[pl.BlockSpec((None, rows, width), fwd),
                  pl.BlockSpec((None, rows, LANES), fwd),
                  pl.BlockSpec((None, rows, width), bwd),
                  pl.BlockSpec((None, rows, LANES), bwd),
                  _resident(conv_w.shape),
                  _resident(alog_row.shape),
                  _resident(dtb_row.shape),
                  _resident(expand.shape),
                  sspec],
        out_specs=[pl.BlockSpec((None, rows, wide), fwd),
                   pl.BlockSpec((None, rows, wide), bwd),
                   sspec],
        scratch_shapes=[pltpu.VMEM((2, GDN_HEADS, GDN_D, GDN_D), F32),
                        pltpu.VMEM((rows + 2 * CONV_PAD, width), F32),
                        pltpu.VMEM((2, rows, width), BF16),
                        pltpu.VMEM((2, rows, wide), F32),
                        pltpu.VMEM((2, nchunk, LANES, CHUNK), F32),
                        pltpu.VMEM((2, nchunk, LANES, CHUNK), F32)],
        compiler_params=_params(2),
        name="gdn",
    )(dqkv, sm, dqkv, sm, conv_w, alog_row, dtb_row, expand, s0)


def _head_norm(o, heads, gain):
    dv = o.shape[-1] // heads
    parts = []
    for h in range(heads):
        oh = o[:, h * dv:(h + 1) * dv]
        parts.append(oh * lax.rsqrt(jnp.mean(oh * oh, axis=-1, keepdims=True) + EPS))
    return jnp.concatenate(parts, axis=-1) * gain


def _tail_kernel(x_ref, glaf, glab, gz, gdnf, gdnb, dz, og, gate1, shift2, scale2, gate2,
                 gla_gain, gdn_gain, n2w, fw, w_out, w_gate, w_up, w_down, out_ref):
    d = x_ref.shape[-1]
    y_gla = _head_norm(glaf[...] + glab[...], GLA_HEADS, gla_gain[...]) * _silu(gz[...])
    y_gdn = _head_norm(gdnf[...] + gdnb[...], GDN_HEADS, gdn_gain[...]) * _silu(dz[...])
    merged = _sigmoid(og[:, :d]) * y_gla + _sigmoid(og[:, d:]) * y_gdn
    x1 = x_ref[...] + gate1[...] * _dot(merged.astype(BF16), w_out[...])
    h2 = (_rms(x1, n2w[...]) * (1.0 + scale2[...]) + shift2[...]).astype(BF16)
    hidden = _silu(_dot(h2, w_gate[...])) * _dot(h2, w_up[...])
    x2 = x1 + gate2[...] * _dot(hidden.astype(BF16), w_down[...])
    out_ref[...] = _rms(x2, fw[...])


def _tail_call(x2d, glaf, glab, gz, gdnf, gdnb, dz, og, mod3, gla_gain, gdn_gain, n2w, fw,
               w_out, w_gate, w_up, w_down, tokens_per_mod, tm):
    n, d = x2d.shape
    mod_row = lambda i: (i * tm) // tokens_per_mod
    tok = lambda w: pl.BlockSpec((tm, w), lambda i: (i, 0))
    modspec = lambda j: pl.BlockSpec((None, 1, d), lambda i: (mod_row(i), 0, j))
    return pl.pallas_call(
        _tail_kernel,
        out_shape=jax.ShapeDtypeStruct((n, d), F32),
        grid=(n // tm,),
        in_specs=[tok(d)] * 7 + [tok(2 * d)]
                 + [modspec(2), modspec(3), modspec(4), modspec(5)]
                 + [_resident(a.shape) for a in (gla_gain, gdn_gain, n2w, fw, w_out, w_gate, w_up, w_down)],
        out_specs=tok(d),
        compiler_params=_params(1),
        name="tail",
    )(x2d, glaf, glab, gz, gdnf, gdnb, dz, og, mod3, mod3, mod3, mod3,
      gla_gain, gdn_gain, n2w, fw, w_out, w_gate, w_up, w_down)


def _permute_w_in(w_in):
    d = w_in.shape[0]
    gq, gk, gv, gz, lrf, lrb, dq, dk, dv, dz, ab, p11, p12 = jnp.split(
        w_in, np.cumsum([512, 512, 1024, 1024, 16, 16, 1024, 1024, 1024, 1024, 32, 1024])[:].tolist(), axis=1)
    pad = jnp.zeros((d, LANES - 2 * GLA_LOWRANK - 4 * GDN_HEADS), w_in.dtype)
    return jnp.concatenate([gq, gk, gv, gz, dq, dk, dv, dz, p11, p12, lrf, lrb, ab, pad], axis=1)


def _gdn_expand():
    e = np.zeros((2, LANES, GDN_HEADS * GDN_D), np.float32)
    for d in range(2):
        for h in range(GDN_HEADS):
            e[d, SM_A + GDN_HEADS * d + h, h * GDN_D:(h + 1) * GDN_D] = 1.0
    return jnp.asarray(e, dtype=BF16)


def kernel(x, c, ctx, c_ctx, w_mod, b_mod, norm1_w, norm2_w, w_in, gla_lr_w, gla_lr_b, gla_norm_w, gdn_conv_w, gdn_a_log, gdn_dt_bias, gdn_norm_w, w_out, ffn_w_gate, ffn_w_up, ffn_w_down, final_norm_w):
    b, seq, d = x.shape
    ctx_len = ctx.shape[1]
    depth = w_mod.shape[0]
    assert depth == 1, "only the single-layer block is implemented"
    assert d == GLA_HEADS * GLA_DV == GDN_HEADS * GDN_D
    rows = seq // GRID_W
    assert seq % GRID_W == 0 and rows % CHUNK == 0 and ctx_len % CHUNK == 0
    layer = 0

    mod_rows = -(-(b + 1) // 8) * 8
    cc = jnp.zeros((mod_rows, d), F32).at[:b].set(c).at[b].set(c_ctx)
    mod = _mod_call(cc, w_mod[layer], b_mod[layer][None, :])
    mod_l = mod[:b, None, :]
    mod_c = mod[b:b + 1, None, :]

    w_perm = _permute_w_in(w_in[layer]).astype(BF16)
    n1w = norm1_w[layer][None, :]
    tm = 256
    proj_l = _inproj_call(x.reshape(b * seq, d), mod_l, n1w, w_perm, seq, tm)
    proj_c = _inproj_call(ctx.reshape(b * ctx_len, d), mod_c, n1w, w_perm, b * ctx_len, tm)
    gqk_l, gv_l, gz_l, dqkv_l, dz_l, og_l, sm_l = proj_l
    gqk_c, gv_c, _, dqkv_c, _, _, sm_c = proj_c

    lrw_pad = jnp.zeros((2, LANES, GLA_HEADS * GLA_DK), F32)
    lrw_pad = lrw_pad.at[0, SM_LRF:SM_LRF + GLA_LOWRANK].set(gla_lr_w[layer, 0])
    lrw_pad = lrw_pad.at[1, SM_LRB:SM_LRB + GLA_LOWRANK].set(gla_lr_w[layer, 1])
    lrb = gla_lr_b[layer][:, None, :]
    r3 = lambda t, l: t.reshape(b, l, t.shape[-1])
    s0_gla = jnp.zeros((b, 2, GLA_HEADS, GLA_DK, GLA_DV), F32)
    _, _, s_gla = _gla_call(r3(gqk_c, ctx_len), r3(gv_c, ctx_len), r3(sm_c, ctx_len), lrw_pad, lrb, s0_gla, ctx_len)
    gla_f, gla_b, _ = _gla_call(r3(gqk_l, seq), r3(gv_l, seq), r3(sm_l, seq), lrw_pad, lrb, s_gla, 256)

    lane_params = lambda p: jnp.zeros((1, LANES), F32).at[0, SM_A:SM_A + 2 * GDN_HEADS].set(p[layer].reshape(-1))
    alog_row, dtb_row = lane_params(gdn_a_log), lane_params(gdn_dt_bias)
    expand = _gdn_expand()
    conv_w = gdn_conv_w[layer]
    s0_gdn = jnp.zeros((b, 2, GDN_HEADS, GDN_D, GDN_D), F32)
    _, _, s_gdn = _gdn_call(r3(dqkv_c, ctx_len), r3(sm_c, ctx_len), conv_w, alog_row, dtb_row, expand, s0_gdn, ctx_len)
    col = lambda t: t.reshape(b, rows, GRID_W * t.shape[-1])
    gdn_f, gdn_b, _ = _gdn_call(col(dqkv_l), col(sm_l), conv_w, alog_row, dtb_row, expand, s_gdn, rows)

    flat = lambda t: t.reshape(b * seq, d)
    out = _tail_call(
        x.reshape(b * seq, d), flat(gla_f), flat(gla_b), gz_l, flat(gdn_f), flat(gdn_b), dz_l, og_l, mod_l,
        gla_norm_w[layer].reshape(1, d), gdn_norm_w[layer].reshape(1, d),
        norm2_w[layer][None, :], final_norm_w[None, :],
        w_out[layer].astype(BF16), ffn_w_gate[layer].astype(BF16), ffn_w_up[layer].astype(BF16),
        ffn_w_down[layer].astype(BF16), seq, tm)
    return out.reshape(b, seq, d)
```

```python
import functools

import numpy as np
import jax
import jax.numpy as jnp
from jax import lax
from jax.experimental import pallas as pl
from jax.experimental.pallas import tpu as pltpu

F32 = jnp.float32
BF16 = jnp.bfloat16
HI = lax.Precision.HIGHEST

EPS = 1e-6
CHUNK = 64
GRID_W = 64
CONV_W = 5
CONV_PAD = 8

GLA_HEADS = 4
GLA_DK = 128
GLA_DV = 256
GLA_LOWRANK = 16
GLA_GATE_NORM = 16.0
GDN_HEADS = 8
GDN_D = 128

LANES = 128
VMEM_LIMIT = 56 * 1024 * 1024

SM_LRF, SM_LRB, SM_A, SM_B = 0, 16, 32, 48


def _dot(a, b, **kw):
    return jnp.dot(a, b, preferred_element_type=F32, **kw)


def _dot_nt(a, b):
    return lax.dot_general(a, b, (((1,), (1,)), ((), ())), preferred_element_type=F32)


def _dot_tn(a, b):
    return lax.dot_general(a, b, (((0,), (0,)), ((), ())), preferred_element_type=F32)


def _sigmoid(t):
    return 1.0 / (1.0 + jnp.exp(-t))


def _silu(t):
    return t * _sigmoid(t)


def _softplus(t):
    return jnp.maximum(t, 0.0) + jnp.log(1.0 + jnp.exp(-jnp.abs(t)))


def _rms(t, w):
    return t * lax.rsqrt(jnp.mean(t * t, axis=-1, keepdims=True) + EPS) * w


def _params(n_grid):
    return pltpu.CompilerParams(dimension_semantics=("arbitrary",) * n_grid,
                                vmem_limit_bytes=VMEM_LIMIT)


def _resident(shape):
    return pl.BlockSpec(shape, lambda *_: (0,) * len(shape), pipeline_mode=pl.Buffered(1))


def _mod_kernel(c_ref, w_ref, b_ref, o_ref):
    o_ref[...] = _dot(_silu(c_ref[...]), w_ref[...], precision=HI) + b_ref[...]


def _mod_call(cc, w_mod, b_mod):
    rows, d = cc.shape
    n = w_mod.shape[1]
    return pl.pallas_call(
        _mod_kernel,
        out_shape=jax.ShapeDtypeStruct((rows, n), F32),
        grid=(n // d,),
        in_specs=[pl.BlockSpec((rows, d), lambda j: (0, 0)),
                  pl.BlockSpec((d, d), lambda j: (0, j)),
                  pl.BlockSpec((1, d), lambda j: (0, j))],
        out_specs=pl.BlockSpec((rows, d), lambda j: (0, j)),
        compiler_params=_params(1),
        name="mod",
    )(cc, w_mod, b_mod)


def _inproj_kernel(x_ref, shift_ref, scale_ref, nw_ref, w_ref, *out_refs):
    h = _rms(x_ref[...], nw_ref[...]) * (1.0 + scale_ref[...]) + shift_ref[...]
    hb = h.astype(BF16)
    lo = 0
    for ref in out_refs:
        width = ref.shape[-1]
        ref[...] = _dot(hb, w_ref[:, lo:lo + width])
        lo += width


def _inproj_call(x2d, mod3, norm_w, weights, tokens_per_mod, tm):
    n, d = x2d.shape
    widths = [w.shape[1] for w in weights]
    w_all = jnp.concatenate(weights, axis=1)
    mod_row = lambda i: (i * tm) // tokens_per_mod
    return pl.pallas_call(
        _inproj_kernel,
        out_shape=[jax.ShapeDtypeStruct((n, w), F32) for w in widths],
        grid=(n // tm,),
        in_specs=[pl.BlockSpec((tm, d), lambda i: (i, 0)),
                  pl.BlockSpec((None, 1, d), lambda i: (mod_row(i), 0, 0)),
                  pl.BlockSpec((None, 1, d), lambda i: (mod_row(i), 0, 1)),
                  _resident((1, d)),
                  _resident(w_all.shape)],
        out_specs=[pl.BlockSpec((tm, w), lambda i: (i, 0)) for w in widths],
        compiler_params=_params(1),
        name="inproj",
    )(x2d, mod3, mod3, norm_w, w_all)


GPROJ_COLS = 8
GPROJ_WIDTH = 2 * LANES


def _gproj_kernel(x_ref, shift_ref, scale_ref, nw_ref, w_ref, convw_ref, qkv_ref, smg_ref,
                  pad_ref, y_ref, *, rows):
    nq = GDN_HEADS * GDN_D
    n = rows * GPROJ_COLS
    halo = (CONV_W // 2) * GPROJ_COLS
    x = x_ref[...].reshape(n, x_ref.shape[-1])
    hb = (_rms(x, nw_ref[...]) * (1.0 + scale_ref[...]) + shift_ref[...]).astype(BF16)
    zeros = jnp.zeros((halo, GPROJ_WIDTH), F32)
    pad_ref[0:halo, :] = zeros
    pad_ref[halo + n:halo + n + halo, :] = zeros
    column = lambda slab, c: y_ref[slab, pl.ds(c, rows, stride=GPROJ_COLS), :]
    for lo in range(0, 3 * nq, GPROJ_WIDTH):
        pad_ref[halo:halo + n, :] = _dot(hb, w_ref[:, lo:lo + GPROJ_WIDTH])
        acc = None
        for i in range(CONV_W):
            start = halo + (i - CONV_W // 2) * GPROJ_COLS
            term = convw_ref[i:i + 1, lo:lo + GPROJ_WIDTH] * pad_ref[start:start + n, :]
            acc = term if acc is None else acc + term
        y = _silu(acc)
        for slab in range(GPROJ_WIDTH // LANES):
            yh = y[:, slab * LANES:(slab + 1) * LANES]
            if lo < 2 * nq:
                scale = GDN_D ** -0.5 if lo < nq else 1.0
                yh = yh * (lax.rsqrt(jnp.sum(yh * yh, axis=-1, keepdims=True) + EPS) * scale)
            y_ref[slab] = yh
        for c in range(GPROJ_COLS):
            for slab in range(GPROJ_WIDTH // LANES):
                qkv_ref[c, :, lo + slab * LANES:lo + (slab + 1) * LANES] = column(slab, c).astype(BF16)
    y_ref[0] = _dot(hb, w_ref[:, 3 * nq:3 * nq + LANES])
    for c in range(GPROJ_COLS):
        smg_ref[c] = column(0, c)


def _gproj_call(x4, mod3, norm_w, w_g, conv_w):
    b, rows, ncol, d = x4.shape
    width = 3 * GDN_HEADS * GDN_D
    assert GDN_D == LANES and ncol % GPROJ_COLS == 0
    n = rows * GPROJ_COLS
    halo = (CONV_W // 2) * GPROJ_COLS
    return pl.pallas_call(
        functools.partial(_gproj_kernel, rows=rows),
        out_shape=[jax.ShapeDtypeStruct((b, ncol, rows, width), BF16),
                   jax.ShapeDtypeStruct((b, ncol, rows, LANES), F32)],
        grid=(b, ncol // GPROJ_COLS),
        in_specs=[pl.BlockSpec((None, rows, GPROJ_COLS, d), lambda bi, j: (bi, 0, j, 0)),
                  pl.BlockSpec((None, 1, d), lambda bi, j: (bi, 0, 0)),
                  pl.BlockSpec((None, 1, d), lambda bi, j: (bi, 0, 1)),
                  _resident((1, d)),
                  _resident(w_g.shape),
                  _resident(conv_w.shape)],
        out_specs=[pl.BlockSpec((None, GPROJ_COLS, rows, width), lambda bi, j: (bi, j, 0, 0)),
                   pl.BlockSpec((None, GPROJ_COLS, rows, LANES), lambda bi, j: (bi, j, 0, 0))],
        scratch_shapes=[pltpu.VMEM((n + 2 * halo, GPROJ_WIDTH), F32),
                        pltpu.VMEM((GPROJ_WIDTH // LANES, n, LANES), F32)],
        compiler_params=_params(2),
        name="gproj",
    )(x4, mod3, mod3, norm_w, w_g, conv_w)


def _tri(rev):
    row = lax.broadcasted_iota(jnp.int32, (CHUNK, CHUNK), 0)
    col = lax.broadcasted_iota(jnp.int32, (CHUNK, CHUNK), 1)
    return (col >= row) if rev else (col <= row)


def _split3(t):
    hi = t.astype(BF16)
    r1 = t - hi.astype(F32)
    mid = r1.astype(BF16)
    lo = (r1 - mid.astype(F32)).astype(BF16)
    return hi, mid, lo


def _dot3(a, b):
    a1 = a.astype(BF16)
    a2 = (a - a1.astype(F32)).astype(BF16)
    b1 = b.astype(BF16)
    b2 = (b - b1.astype(F32)).astype(BF16)
    return _dot(a1, b1) + _dot(a1, b2) + _dot(a2, b1)


def _gla_gates(d, sm_ref, lrw_ref, lrb_ref, g_ref):
    tb = sm_ref.shape[0]
    gpre = _dot3(sm_ref[...], lrw_ref[d]) + lrb_ref[d]
    g = (jnp.minimum(gpre, 0.0) - jnp.log(1.0 + jnp.exp(-jnp.abs(gpre)))) * (1.0 / GLA_GATE_NORM)
    row = lax.broadcasted_iota(jnp.int32, (tb, tb), 0)
    col = lax.broadcasted_iota(jnp.int32, (tb, tb), 1)
    order = (col >= row) if d == 1 else (col <= row)
    same_chunk = (row >> 6) == (col >> 6)
    tri = jnp.where(same_chunk, jnp.where(order, 1.0, 0.0), 0.0).astype(BF16)
    hi, mid, lo = _split3(g)
    g_ref[d] = _dot(tri, hi) + _dot(tri, mid) + _dot(tri, lo)


def _gla_local(d, c, qk_ref, v_ref, g_ref):
    rev = d == 1
    incl = _tri(rev)
    rows = slice(c * CHUNK, (c + 1) * CHUNK)
    qk = qk_ref[rows, :]
    v = v_ref[rows, :]
    G = g_ref[d, rows, :]
    g_mid = G[CHUNK // 2:CHUNK // 2 + 1, :]
    g_end = G[0:1, :] if rev else G[CHUNK - 1:CHUNK, :]
    e_g = jnp.exp(G)
    e_q = jnp.exp(G - g_mid)
    e_k = jnp.exp(g_mid - G)
    e_kend = jnp.exp(g_end - G)
    e_end = jnp.exp(g_end)
    nqk = GLA_HEADS * GLA_DK
    st = []
    for h in range(GLA_HEADS):
        sl = slice(h * GLA_DK, (h + 1) * GLA_DK)
        q = qk[:, sl] * (GLA_DK ** -0.5)
        k = qk[:, nqk + h * GLA_DK:nqk + (h + 1) * GLA_DK]
        st.append(dict(d=d, h=h, rows=rows, incl=incl,
                       v=v[:, h * GLA_DV:(h + 1) * GLA_DV].astype(BF16),
                       qt=(q * e_q[:, sl]).astype(BF16), kt=(k * e_k[:, sl]).astype(BF16),
                       qg=(q * e_g[:, sl]).astype(BF16), ke=(k * e_kend[:, sl]).astype(BF16),
                       e_end=e_end[:, sl]))
    for s in st:
        s["a"] = jnp.where(s["incl"], _dot_nt(s["qt"], s["kt"]), 0.0).astype(BF16)
    for s in st:
        s["kv"] = _dot_tn(s["ke"], s["v"])
    return st


def _gla_recur(st, o_refs, s_ref):
    for s in st:
        s["s"] = s_ref[s["d"], s["h"]]
        s["o"] = _dot(s["a"], s["v"]) + _dot(s["qg"], s["s"].astype(BF16))
    for s in st:
        h = s["h"]
        o_refs[s["d"]][s["rows"], h * GLA_DV:(h + 1) * GLA_DV] = s["o"]
        decay = jnp.broadcast_to(s["e_end"], (GLA_DK, GLA_DK)).T
        decay = jnp.concatenate([decay] * (GLA_DV // GLA_DK), axis=1)
        s_ref[s["d"], h] = decay * s["s"] + s["kv"]


def _gla_kernel(qkf, vf, smf, qkb, vb, smb, lrw, lrb, s0, of, ob, sfin, s_ref, g_ref, *, nchunk):
    i = pl.program_id(1)

    @pl.when(i == 0)
    def _():
        s_ref[...] = s0[...]

    _gla_gates(0, smf, lrw, lrb, g_ref)
    _gla_gates(1, smb, lrw, lrb, g_ref)
    local = lambda t: (_gla_local(0, t, qkf, vf, g_ref) + _gla_local(1, nchunk - 1 - t, qkb, vb, g_ref))
    pending = local(0)
    for t in range(nchunk):
        ahead = local(t + 1) if t + 1 < nchunk else None
        _gla_recur(pending, (of, ob), s_ref)
        pending = ahead

    @pl.when(i == pl.num_programs(1) - 1)
    def _():
        sfin[...] = s_ref[...]


def _gla_call(gqk, gv, sm, lrw_pad, lrb, s0, tb):
    b, l, _ = gqk.shape
    nblk = l // tb
    fwd = lambda bi, i: (bi, i, 0)
    bwd = lambda bi, i: (bi, nblk - 1 - i, 0)
    sspec = pl.BlockSpec((None, 2, GLA_HEADS, GLA_DK, GLA_DV), lambda bi, i: (bi, 0, 0, 0, 0))
    wide = GLA_HEADS * GLA_DV
    return pl.pallas_call(
        functools.partial(_gla_kernel, nchunk=tb // CHUNK),
        out_shape=[jax.ShapeDtypeStruct((b, l, wide), F32),
                   jax.ShapeDtypeStruct((b, l, wide), F32),
                   jax.ShapeDtypeStruct(s0.shape, F32)],
        grid=(b, nblk),
        in_specs=[pl.BlockSpec((None, tb, 2 * GLA_HEADS * GLA_DK), fwd),
                  pl.BlockSpec((None, tb, wide), fwd),
                  pl.BlockSpec((None, tb, LANES), fwd),
                  pl.BlockSpec((None, tb, 2 * GLA_HEADS * GLA_DK), bwd),
                  pl.BlockSpec((None, tb, wide), bwd),
                  pl.BlockSpec((None, tb, LANES), bwd),
                  _resident(lrw_pad.shape),
                  _resident(lrb.shape),
                  sspec],
        out_specs=[pl.BlockSpec((None, tb, wide), fwd),
                   pl.BlockSpec((None, tb, wide), bwd),
                   sspec],
        scratch_shapes=[pltpu.VMEM((2, GLA_HEADS, GLA_DK, GLA_DV), F32),
                        pltpu.VMEM((2, tb, GLA_HEADS * GLA_DK), F32)],
        compiler_params=_params(2),
        name="gla",
    )(gqk, gv, sm, gqk, gv, sm, lrw_pad, lrb, s0)


def _gdn_conv(x_ref, convw_ref, xp_ref, qkv_ref):
    rows = x_ref.shape[0]
    nq = GDN_HEADS * GDN_D
    width = 3 * nq
    zeros = jnp.zeros((CONV_PAD, width), F32)
    xp_ref[0:CONV_PAD, :] = zeros
    xp_ref[CONV_PAD + rows:CONV_PAD + rows + CONV_PAD, :] = zeros
    xp_ref[CONV_PAD:CONV_PAD + rows, :] = x_ref[...]
    acc = None
    for i in range(CONV_W):
        start = CONV_PAD + i - CONV_W // 2
        term = convw_ref[i:i + 1, :] * xp_ref[start:start + rows, :]
        acc = term if acc is None else acc + term
    xs = _silu(acc)
    for h in range(GDN_HEADS):
        qh = xs[:, h * GDN_D:(h + 1) * GDN_D]
        kh = xs[:, nq + h * GDN_D:nq + (h + 1) * GDN_D]
        qn = qh * (lax.rsqrt(jnp.sum(qh * qh, axis=-1, keepdims=True) + EPS) * GDN_D ** -0.5)
        kn = kh * lax.rsqrt(jnp.sum(kh * kh, axis=-1, keepdims=True) + EPS)
        qkv_ref[:, h * GDN_D:(h + 1) * GDN_D] = qn.astype(BF16)
        qkv_ref[:, nq + h * GDN_D:nq + (h + 1) * GDN_D] = kn.astype(BF16)
    qkv_ref[:, 2 * nq:] = xs[:, 2 * nq:].astype(BF16)


def _gdn_gates(d, sm_ref, alog_ref, dtb_ref, expand_ref, gx_ref, gt_ref, bt_ref):
    rev = d == 1
    rows = sm_ref.shape[0]
    sm = sm_ref[...]
    gate = -jnp.exp(alog_ref[...]) * _softplus(sm + dtb_ref[...])
    tri = jnp.where(_tri(rev), 1.0, 0.0).astype(BF16)
    nchunk = rows // CHUNK
    hi, mid, lo = _split3(gate)
    cum = lambda c: sum(_dot(tri, piece[c * CHUNK:(c + 1) * CHUNK, :]) for piece in (hi, mid, lo))
    gcum = jnp.concatenate([cum(c) for c in range(nchunk)], axis=0)
    onehot = expand_ref[d]
    hi, mid, lo = _split3(gcum)
    gx_ref[d] = _dot(hi, onehot) + _dot(mid, onehot) + _dot(lo, onehot)
    g_t = gcum.T
    b_t = _sigmoid(sm).T
    for c in range(nchunk):
        gt_ref[d, c] = g_t[:, c * CHUNK:(c + 1) * CHUNK]
        bt_ref[d, c] = b_t[:, c * CHUNK:(c + 1) * CHUNK]


def _gdn_local(chains, qkv_refs, gx_ref, gt_ref, bt_ref):
    nq = GDN_HEADS * GDN_D
    row = lax.broadcasted_iota(jnp.int32, (CHUNK, CHUNK), 0)
    col = lax.broadcasted_iota(jnp.int32, (CHUNK, CHUNK), 1)
    eye = row == col
    eye_f = eye.astype(F32)
    st = []
    for d, h, c in chains:
        rev = d == 1
        rs = slice(c * CHUNK, (c + 1) * CHUNK)
        lane_g = SM_A + GDN_HEADS * d + h
        lane_b = SM_B + GDN_HEADS * d + h
        end = c * CHUNK if rev else (c + 1) * CHUNK - 1
        qkv_ref = qkv_refs[d]
        q = qkv_ref[rs, h * GDN_D:(h + 1) * GDN_D]
        k = qkv_ref[rs, nq + h * GDN_D:nq + (h + 1) * GDN_D]
        st.append(dict(
            d=d, h=h, c=c, rs=rs, k=k,
            strict=(col > row) if rev else (col < row),
            qk=jnp.concatenate([q, k], axis=0),
            v=qkv_ref[rs, 2 * nq + h * GDN_D:2 * nq + (h + 1) * GDN_D],
            gj=gt_ref[d, c, lane_g:lane_g + 1, :],
            bj=bt_ref[d, c, lane_b:lane_b + 1, :],
            gi=gx_ref[d, rs, h * GDN_D:h * GDN_D + CHUNK],
            g_end=gx_ref[d, end:end + 1, h * GDN_D:(h + 1) * GDN_D]))
    for s in st:
        s["qkk"] = _dot_nt(s["qk"], s["k"])
    for s in st:
        dec = jnp.exp(jnp.where(s["strict"], s["gi"] - s["gj"], -1e30))
        s["dec"] = dec
        x = -(s["qkk"][CHUNK:] * dec * s["bj"])
        s["x"] = x
        s["p"] = eye_f + jnp.where((row >> 1) == (col >> 1), x, 0.0)
    for level in range(1, 6):
        sibling = ((row >> level) ^ 1) == (col >> level)
        for s in st:
            s["pb"] = s["p"].astype(BF16)
            s["t"] = _dot(jnp.where(sibling, s["x"], 0.0).astype(BF16), s["pb"])
        for s in st:
            s["p"] = s["p"] + _dot(s["pb"], s["t"].astype(BF16))
    for s in st:
        rb = s["p"].astype(BF16)
        s["rv"] = _dot(rb, s["v"])
    for s in st:
        e_gj = jnp.exp(s["gj"])
        s["rg"] = (s["p"] * e_gj).astype(BF16)
        s["e"] = jnp.where(eye, e_gj, 0.0).astype(BF16)
        a_beta = s["qkk"][:CHUNK] * (s["dec"] + eye_f) * s["bj"]
        kscale = jnp.where(eye, jnp.exp(s["g_end"][:, :CHUNK] - s["gj"]) * s["bj"], 0.0)
        s["ak"] = jnp.concatenate([a_beta, kscale], axis=0).astype(BF16)
    return st


def _gdn_recur(st, o_refs, s_ref):
    for s in st:
        s["s"] = s_ref[s["d"], s["h"]]
        s["qks"] = _dot(s["qk"], s["s"].astype(BF16))
    for s in st:
        s["vhat"] = (s["rv"] - _dot(s["rg"], s["qks"][CHUNK:].astype(BF16))).astype(BF16)
    for s in st:
        s["av"] = _dot(s["ak"], s["vhat"])
        s["eq"] = _dot(s["e"], s["qks"][:CHUNK].astype(BF16))
    for s in st:
        h = s["h"]
        o_refs[s["d"]][s["rs"], h * GDN_D:(h + 1) * GDN_D] = s["eq"] + s["av"][:CHUNK]
        s_ref[s["d"], h] = jnp.exp(s["g_end"]) * s["s"] + _dot_tn(s["k"], s["av"][CHUNK:].astype(BF16))


def _gdn_scan(qkv_refs, smf, smb, alog, dtb, expand, s0, of, ob, sfin, s_ref, gx_ref, gt_ref, bt_ref):
    i = pl.program_id(1)

    @pl.when(i == 0)
    def _():
        s_ref[...] = s0[...]

    _gdn_gates(0, smf, alog, dtb, expand, gx_ref, gt_ref, bt_ref)
    _gdn_gates(1, smb, alog, dtb, expand, gx_ref, gt_ref, bt_ref)
    nchunk = smf.shape[0] // CHUNK
    for step in range(nchunk):
        chains = [(0, h, step) for h in range(GDN_HEADS)] + [(1, h, nchunk - 1 - step) for h in range(GDN_HEADS)]
        _gdn_recur(_gdn_local(chains, qkv_refs, gx_ref, gt_ref, bt_ref), (of, ob), s_ref)

    @pl.when(i == pl.num_programs(1) - 1)
    def _():
        sfin[...] = s_ref[...]


def _gdn_kernel(qkvf, smf, qkvb, smb, alog, dtb, expand, s0, of, ob, sfin, s_ref, gx_ref, gt_ref, bt_ref):
    _gdn_scan((qkvf, qkvb), smf, smb, alog, dtb, expand, s0, of, ob, sfin, s_ref, gx_ref, gt_ref, bt_ref)


def _gdn_raw_kernel(xf, smf, xb, smb, convw, alog, dtb, expand, s0, of, ob, sfin,
                    s_ref, gx_ref, gt_ref, bt_ref, xp_ref, qkv_ref):
    _gdn_conv(xf, convw, xp_ref, qkv_ref.at[0])
    _gdn_conv(xb, convw, xp_ref, qkv_ref.at[1])
    _gdn_scan((qkv_ref.at[0], qkv_ref.at[1]), smf, smb, alog, dtb, expand, s0, of, ob, sfin,
              s_ref, gx_ref, gt_ref, bt_ref)


def _gdn_call(qkv, sm, alog_row, dtb_row, expand, s0, conv_w=None):
    b, ncol, rows, width = qkv.shape
    wide = GDN_HEADS * GDN_D
    nchunk = rows // CHUNK
    fwd = lambda bi, i: (bi, i, 0, 0)
    bwd = lambda bi, i: (bi, ncol - 1 - i, 0, 0)
    slab = lambda w, index: pl.BlockSpec((None, None, rows, w), index)
    sspec = pl.BlockSpec((None, 2, GDN_HEADS, GDN_D, GDN_D), lambda bi, i: (bi, 0, 0, 0, 0))
    params = [alog_row, dtb_row, expand]
    scratch = [pltpu.VMEM((2, GDN_HEADS, GDN_D, GDN_D), F32),
               pltpu.VMEM((2, rows, wide), F32),
               pltpu.VMEM((2, nchunk, LANES, CHUNK), F32),
               pltpu.VMEM((2, nchunk, LANES, CHUNK), F32)]
    body = _gdn_kernel
    if conv_w is not None:
        params = [conv_w] + params
        scratch += [pltpu.VMEM((rows + 2 * CONV_PAD, width), F32),
                    pltpu.VMEM((2, rows, width), BF16)]
        body = _gdn_raw_kernel
    return pl.pallas_call(
        body,
        out_shape=[jax.ShapeDtypeStruct((b, ncol, rows, wide), F32),
                   jax.ShapeDtypeStruct((b, ncol, rows, wide), F32),
                   jax.ShapeDtypeStruct(s0.shape, F32)],
        grid=(b, ncol),
        in_specs=[slab(width, fwd), slab(LANES, fwd), slab(width, bwd), slab(LANES, bwd)]
                 + [_resident(p.shape) for p in params] + [sspec],
        out_specs=[slab(wide, fwd), slab(wide, bwd), sspec],
        scratch_shapes=scratch,
        compiler_params=_params(2),
        name="gdn",
    )(qkv, sm, qkv, sm, *params, s0)


def _head_norm(o, heads, gain):
    dv = o.shape[-1] // heads
    parts = []
    for h in range(heads):
        oh = o[:, h * dv:(h + 1) * dv]
        parts.append(oh * lax.rsqrt(jnp.mean(oh * oh, axis=-1, keepdims=True) + EPS))
    return jnp.concatenate(parts, axis=-1) * gain


def _tail_kernel(x_ref, glaf, glab, gz, gdnf, gdnb, dz, og, gate1, shift2, scale2, gate2,
                 gla_gain, gdn_gain, n2w, fw, w_out, w_gate, w_up, w_down, out_ref):
    d = x_ref.shape[-1]
    y_gla = _head_norm(glaf[...] + glab[...], GLA_HEADS, gla_gain[...]) * _silu(gz[...])
    y_gdn = _head_norm(gdnf[...] + gdnb[...], GDN_HEADS, gdn_gain[...]) * _silu(dz[...])
    merged = _sigmoid(og[:, :d]) * y_gla + _sigmoid(og[:, d:]) * y_gdn
    x1 = x_ref[...] + gate1[...] * _dot(merged.astype(BF16), w_out[...])
    h2 = (_rms(x1, n2w[...]) * (1.0 + scale2[...]) + shift2[...]).astype(BF16)
    hidden = _silu(_dot(h2, w_gate[...])) * _dot(h2, w_up[...])
    x2 = x1 + gate2[...] * _dot(hidden.astype(BF16), w_down[...])
    out_ref[...] = _rms(x2, fw[...])


def _tail_call(x2d, glaf, glab, gz, gdnf, gdnb, dz, og, mod3, gla_gain, gdn_gain, n2w, fw,
               w_out, w_gate, w_up, w_down, tokens_per_mod, tm):
    n, d = x2d.shape
    mod_row = lambda i: (i * tm) // tokens_per_mod
    tok = lambda w: pl.BlockSpec((tm, w), lambda i: (i, 0))
    modspec = lambda j: pl.BlockSpec((None, 1, d), lambda i: (mod_row(i), 0, j))
    return pl.pallas_call(
        _tail_kernel,
        out_shape=jax.ShapeDtypeStruct((n, d), F32),
        grid=(n // tm,),
        in_specs=[tok(d)] * 7 + [tok(2 * d)]
                 + [modspec(2), modspec(3), modspec(4), modspec(5)]
                 + [_resident(a.shape) for a in (gla_gain, gdn_gain, n2w, fw, w_out, w_gate, w_up, w_down)],
        out_specs=tok(d),
        compiler_params=_params(1),
        name="tail",
    )(x2d, glaf, glab, gz, gdnf, gdnb, dz, og, mod3, mod3, mod3, mod3,
      gla_gain, gdn_gain, n2w, fw, w_out, w_gate, w_up, w_down)


def _split_w_in(w_in):
    d = w_in.shape[0]
    gq, gk, gv, gz, lrf, lrb, dq, dk, dv, dz, ab, p11, p12 = jnp.split(
        w_in, np.cumsum([512, 512, 1024, 1024, 16, 16, 1024, 1024, 1024, 1024, 32, 1024]).tolist(), axis=1)
    pad = jnp.zeros((d, LANES - 2 * GLA_LOWRANK - 4 * GDN_HEADS), w_in.dtype)
    cat = lambda *parts: jnp.concatenate(parts, axis=1).astype(BF16)
    return dict(gqk=cat(gq, gk), gv=cat(gv), gz=cat(gz), dqkv=cat(dq, dk, dv), dz=cat(dz),
                og=cat(p11, p12), sm=cat(lrf, lrb, ab, pad))


def _gdn_expand():
    e = np.zeros((2, LANES, GDN_HEADS * GDN_D), np.float32)
    for d in range(2):
        for h in range(GDN_HEADS):
            e[d, SM_A + GDN_HEADS * d + h, h * GDN_D:(h + 1) * GDN_D] = 1.0
    return jnp.asarray(e, dtype=BF16)


def kernel(x, c, ctx, c_ctx, w_mod, b_mod, norm1_w, norm2_w, w_in, gla_lr_w, gla_lr_b, gla_norm_w, gdn_conv_w, gdn_a_log, gdn_dt_bias, gdn_norm_w, w_out, ffn_w_gate, ffn_w_up, ffn_w_down, final_norm_w):
    b, seq, d = x.shape
    ctx_len = ctx.shape[1]
    depth = w_mod.shape[0]
    assert depth == 1, "only the single-layer block is implemented"
    assert d == GLA_HEADS * GLA_DV == GDN_HEADS * GDN_D
    rows = seq // GRID_W
    assert seq % GRID_W == 0 and rows % CHUNK == 0 and ctx_len % CHUNK == 0
    layer = 0

    mod_rows = -(-(b + 1) // 8) * 8
    cc = jnp.zeros((mod_rows, d), F32).at[:b].set(c).at[b].set(c_ctx)
    mod = _mod_call(cc, w_mod[layer], b_mod[layer][None, :])
    mod_l = mod[:b, None, :]
    mod_c = mod[b:b + 1, None, :]

    w = _split_w_in(w_in[layer])
    n1w = norm1_w[layer][None, :]
    conv_w = gdn_conv_w[layer]
    tm = 256
    gqk_l, gv_l, gz_l, dz_l, og_l, sm_l = _inproj_call(
        x.reshape(b * seq, d), mod_l, n1w, [w[k] for k in ("gqk", "gv", "gz", "dz", "og", "sm")], seq, tm)
    gqk_c, gv_c, dqkv_c, sm_c = _inproj_call(
        ctx.reshape(b * ctx_len, d), mod_c, n1w, [w[k] for k in ("gqk", "gv", "dqkv", "sm")], b * ctx_len, tm)
    qkv_l, smg_l = _gproj_call(x.reshape(b, rows, GRID_W, d), mod_l, n1w,
                               jnp.concatenate([w["dqkv"], w["sm"]], axis=1), conv_w)

    lrw_pad = jnp.zeros((2, LANES, GLA_HEADS * GLA_DK), F32)
    lrw_pad = lrw_pad.at[0, SM_LRF:SM_LRF + GLA_LOWRANK].set(gla_lr_w[layer, 0])
    lrw_pad = lrw_pad.at[1, SM_LRB:SM_LRB + GLA_LOWRANK].set(gla_lr_w[layer, 1])
    lrb = gla_lr_b[layer][:, None, :]
    r3 = lambda t, l: t.reshape(b, l, t.shape[-1])
    s0_gla = jnp.zeros((b, 2, GLA_HEADS, GLA_DK, GLA_DV), F32)
    _, _, s_gla = _gla_call(r3(gqk_c, ctx_len), r3(gv_c, ctx_len), r3(sm_c, ctx_len), lrw_pad, lrb, s0_gla, ctx_len)
    gla_f, gla_b, _ = _gla_call(r3(gqk_l, seq), r3(gv_l, seq), r3(sm_l, seq), lrw_pad, lrb, s_gla, 256)

    lane_params = lambda p: jnp.zeros((1, LANES), F32).at[0, SM_A:SM_A + 2 * GDN_HEADS].set(p[layer].reshape(-1))
    alog_row, dtb_row = lane_params(gdn_a_log), lane_params(gdn_dt_bias)
    expand = _gdn_expand()
    s0_gdn = jnp.zeros((b, 2, GDN_HEADS, GDN_D, GDN_D), F32)
    one_col = lambda t: t.reshape(b, 1, ctx_len, t.shape[-1])
    _, _, s_gdn = _gdn_call(one_col(dqkv_c), one_col(sm_c), alog_row, dtb_row, expand, s0_gdn, conv_w=conv_w)
    gdn_f, gdn_b, _ = _gdn_call(qkv_l, smg_l, alog_row, dtb_row, expand, s_gdn)

    flat = lambda t: t.reshape(b * seq, d)
    raster = lambda t: t.transpose(0, 2, 1, 3).reshape(b * seq, d)
    out = _tail_call(
        x.reshape(b * seq, d), flat(gla_f), flat(gla_b), gz_l, raster(gdn_f), raster(gdn_b), dz_l, og_l, mod_l,
        gla_norm_w[layer].reshape(1, d), gdn_norm_w[layer].reshape(1, d),
        norm2_w[layer][None, :], final_norm_w[None, :],
        w_out[layer].astype(BF16), ffn_w_gate[layer].astype(BF16), ffn_w_up[layer].astype(BF16),
        ffn_w_down[layer].astype(BF16), seq, tm)
    return out.reshape(b, seq, d)
```

```python
import functools

import numpy as np
import jax
import jax.numpy as jnp
from jax import lax
from jax.experimental import pallas as pl
from jax.experimental.pallas import tpu as pltpu

F32 = jnp.float32
BF16 = jnp.bfloat16
HI = lax.Precision.HIGHEST

EPS = 1e-6
CHUNK = 64
GRID_W = 64
CONV_W = 5
CONV_PAD = 8

GLA_HEADS = 4
GLA_DK = 128
GLA_DV = 256
GLA_LOWRANK = 16
GLA_GATE_NORM = 16.0
GDN_HEADS = 8
GDN_D = 128

LANES = 128
VMEM_LIMIT = 56 * 1024 * 1024

SM_LRF, SM_LRB, SM_A, SM_B = 0, 16, 32, 48


def _dot(a, b, **kw):
    return jnp.dot(a, b, preferred_element_type=F32, **kw)


def _dot_nt(a, b):
    return lax.dot_general(a, b, (((1,), (1,)), ((), ())), preferred_element_type=F32)


def _dot_tn(a, b):
    return lax.dot_general(a, b, (((0,), (0,)), ((), ())), preferred_element_type=F32)


def _sigmoid(t):
    return 1.0 / (1.0 + jnp.exp(-t))


def _silu(t):
    return t * _sigmoid(t)


def _softplus(t):
    return jnp.maximum(t, 0.0) + jnp.log(1.0 + jnp.exp(-jnp.abs(t)))


def _rms(t, w):
    return t * lax.rsqrt(jnp.mean(t * t, axis=-1, keepdims=True) + EPS) * w


def _params(n_grid):
    return pltpu.CompilerParams(dimension_semantics=("arbitrary",) * n_grid,
                                vmem_limit_bytes=VMEM_LIMIT)


def _resident(shape):
    return pl.BlockSpec(shape, lambda *_: (0,) * len(shape), pipeline_mode=pl.Buffered(1))


def _mod_kernel(c_ref, w_ref, b_ref, o_ref):
    o_ref[...] = _dot(_silu(c_ref[...]), w_ref[...], precision=HI) + b_ref[...]


def _mod_call(cc, w_mod, b_mod):
    rows, d = cc.shape
    n = w_mod.shape[1]
    return pl.pallas_call(
        _mod_kernel,
        out_shape=jax.ShapeDtypeStruct((rows, n), F32),
        grid=(n // d,),
        in_specs=[pl.BlockSpec((rows, d), lambda j: (0, 0)),
                  pl.BlockSpec((d, d), lambda j: (0, j)),
                  pl.BlockSpec((1, d), lambda j: (0, j))],
        out_specs=pl.BlockSpec((rows, d), lambda j: (0, j)),
        compiler_params=_params(1),
        name="mod",
    )(cc, w_mod, b_mod)


def _inproj_kernel(x_ref, shift_ref, scale_ref, nw_ref, w_ref, *out_refs):
    h = _rms(x_ref[...], nw_ref[...]) * (1.0 + scale_ref[...]) + shift_ref[...]
    hb = h.astype(BF16)
    lo = 0
    for ref in out_refs:
        width = ref.shape[-1]
        ref[...] = _dot(hb, w_ref[:, lo:lo + width]).astype(ref.dtype)
        lo += width


def _inproj_call(x2d, mod3, norm_w, weights, tokens_per_mod, tm):
    n, d = x2d.shape
    widths = [w.shape[1] for w in weights]
    w_all = jnp.concatenate(weights, axis=1)
    mod_row = lambda i: (i * tm) // tokens_per_mod
    return pl.pallas_call(
        _inproj_kernel,
        out_shape=[jax.ShapeDtypeStruct((n, w), F32 if w == LANES else BF16) for w in widths],
        grid=(n // tm,),
        in_specs=[pl.BlockSpec((tm, d), lambda i: (i, 0)),
                  pl.BlockSpec((None, 1, d), lambda i: (mod_row(i), 0, 0)),
                  pl.BlockSpec((None, 1, d), lambda i: (mod_row(i), 0, 1)),
                  _resident((1, d)),
                  _resident(w_all.shape)],
        out_specs=[pl.BlockSpec((tm, w), lambda i: (i, 0)) for w in widths],
        compiler_params=_params(1),
        name="inproj",
    )(x2d, mod3, mod3, norm_w, w_all)


GPROJ_COLS = 8
GPROJ_WIDTH = 2 * LANES


def _gproj_kernel(x_ref, shift_ref, scale_ref, nw_ref, w_ref, convw_ref, qkv_ref, smg_ref,
                  pad_ref, y_ref, *, rows):
    nq = GDN_HEADS * GDN_D
    n = rows * GPROJ_COLS
    halo = (CONV_W // 2) * GPROJ_COLS
    x = x_ref[...].reshape(n, x_ref.shape[-1])
    hb = (_rms(x, nw_ref[...]) * (1.0 + scale_ref[...]) + shift_ref[...]).astype(BF16)
    zeros = jnp.zeros((halo, GPROJ_WIDTH), F32)
    pad_ref[0:halo, :] = zeros
    pad_ref[halo + n:halo + n + halo, :] = zeros
    column = lambda slab, c: y_ref[slab, pl.ds(c, rows, stride=GPROJ_COLS), :]
    for lo in range(0, 3 * nq, GPROJ_WIDTH):
        pad_ref[halo:halo + n, :] = _dot(hb, w_ref[:, lo:lo + GPROJ_WIDTH])
        acc = None
        for i in range(CONV_W):
            start = halo + (i - CONV_W // 2) * GPROJ_COLS
            term = convw_ref[i:i + 1, lo:lo + GPROJ_WIDTH] * pad_ref[start:start + n, :]
            acc = term if acc is None else acc + term
        y = _silu(acc)
        for slab in range(GPROJ_WIDTH // LANES):
            yh = y[:, slab * LANES:(slab + 1) * LANES]
            if lo < 2 * nq:
                scale = GDN_D ** -0.5 if lo < nq else 1.0
                yh = yh * (lax.rsqrt(jnp.sum(yh * yh, axis=-1, keepdims=True) + EPS) * scale)
            y_ref[slab] = yh
        for c in range(GPROJ_COLS):
            for slab in range(GPROJ_WIDTH // LANES):
                qkv_ref[c, :, lo + slab * LANES:lo + (slab + 1) * LANES] = column(slab, c).astype(BF16)
    y_ref[0] = _dot(hb, w_ref[:, 3 * nq:3 * nq + LANES])
    for c in range(GPROJ_COLS):
        smg_ref[c] = column(0, c)


def _gproj_call(x4, mod3, norm_w, w_g, conv_w):
    b, rows, ncol, d = x4.shape
    width = 3 * GDN_HEADS * GDN_D
    assert GDN_D == LANES and ncol % GPROJ_COLS == 0
    n = rows * GPROJ_COLS
    halo = (CONV_W // 2) * GPROJ_COLS
    return pl.pallas_call(
        functools.partial(_gproj_kernel, rows=rows),
        out_shape=[jax.ShapeDtypeStruct((b, ncol, rows, width), BF16),
                   jax.ShapeDtypeStruct((b, ncol, rows, LANES), F32)],
        grid=(b, ncol // GPROJ_COLS),
        in_specs=[pl.BlockSpec((None, rows, GPROJ_COLS, d), lambda bi, j: (bi, 0, j, 0)),
                  pl.BlockSpec((None, 1, d), lambda bi, j: (bi, 0, 0)),
                  pl.BlockSpec((None, 1, d), lambda bi, j: (bi, 0, 1)),
                  _resident((1, d)),
                  _resident(w_g.shape),
                  _resident(conv_w.shape)],
        out_specs=[pl.BlockSpec((None, GPROJ_COLS, rows, width), lambda bi, j: (bi, j, 0, 0)),
                   pl.BlockSpec((None, GPROJ_COLS, rows, LANES), lambda bi, j: (bi, j, 0, 0))],
        scratch_shapes=[pltpu.VMEM((n + 2 * halo, GPROJ_WIDTH), F32),
                        pltpu.VMEM((GPROJ_WIDTH // LANES, n, LANES), F32)],
        compiler_params=_params(2),
        name="gproj",
    )(x4, mod3, mod3, norm_w, w_g, conv_w)


def _tri(rev):
    row = lax.broadcasted_iota(jnp.int32, (CHUNK, CHUNK), 0)
    col = lax.broadcasted_iota(jnp.int32, (CHUNK, CHUNK), 1)
    return (col >= row) if rev else (col <= row)


def _split3(t):
    hi = t.astype(BF16)
    r1 = t - hi.astype(F32)
    mid = r1.astype(BF16)
    lo = (r1 - mid.astype(F32)).astype(BF16)
    return hi, mid, lo


def _dot3(a, b):
    a1 = a.astype(BF16)
    a2 = (a - a1.astype(F32)).astype(BF16)
    b1 = b.astype(BF16)
    b2 = (b - b1.astype(F32)).astype(BF16)
    return _dot(a1, b1) + _dot(a1, b2) + _dot(a2, b1)


def _gla_gates(d, sm_ref, lrw_ref, lrb_ref, g_ref):
    tb = sm_ref.shape[0]
    gpre = _dot3(sm_ref[...], lrw_ref[d]) + lrb_ref[d]
    g = (jnp.minimum(gpre, 0.0) - jnp.log(1.0 + jnp.exp(-jnp.abs(gpre)))) * (1.0 / GLA_GATE_NORM)
    row = lax.broadcasted_iota(jnp.int32, (tb, tb), 0)
    col = lax.broadcasted_iota(jnp.int32, (tb, tb), 1)
    order = (col >= row) if d == 1 else (col <= row)
    same_chunk = (row >> 6) == (col >> 6)
    tri = jnp.where(same_chunk, jnp.where(order, 1.0, 0.0), 0.0).astype(BF16)
    hi, mid, lo = _split3(g)
    g_ref[d] = _dot(tri, hi) + _dot(tri, mid) + _dot(tri, lo)


def _gla_local(d, c, qk_ref, v_ref, g_ref):
    rev = d == 1
    incl = _tri(rev)
    rows = slice(c * CHUNK, (c + 1) * CHUNK)
    qk = qk_ref[rows, :].astype(F32)
    v = v_ref[rows, :]
    G = g_ref[d, rows, :]
    g_mid = G[CHUNK // 2:CHUNK // 2 + 1, :]
    g_end = G[0:1, :] if rev else G[CHUNK - 1:CHUNK, :]
    e_g = jnp.exp(G)
    e_q = jnp.exp(G - g_mid)
    e_k = jnp.exp(g_mid - G)
    e_kend = jnp.exp(g_end - G)
    e_end = jnp.exp(g_end)
    nqk = GLA_HEADS * GLA_DK
    st = []
    for h in range(GLA_HEADS):
        sl = slice(h * GLA_DK, (h + 1) * GLA_DK)
        q = qk[:, sl] * (GLA_DK ** -0.5)
        k = qk[:, nqk + h * GLA_DK:nqk + (h + 1) * GLA_DK]
        st.append(dict(d=d, h=h, rows=rows, incl=incl,
                       v=v[:, h * GLA_DV:(h + 1) * GLA_DV],
                       qt=(q * e_q[:, sl]).astype(BF16), kt=(k * e_k[:, sl]).astype(BF16),
                       qg=(q * e_g[:, sl]).astype(BF16), ke=(k * e_kend[:, sl]).astype(BF16),
                       e_end=e_end[:, sl]))
    for s in st:
        s["a"] = jnp.where(s["incl"], _dot_nt(s["qt"], s["kt"]), 0.0).astype(BF16)
    for s in st:
        s["kv"] = _dot_tn(s["ke"], s["v"])
    return st


def _gla_recur(st, o_refs, s_ref):
    for s in st:
        s["s"] = s_ref[s["d"], s["h"]]
        s["o"] = _dot(s["a"], s["v"]) + _dot(s["qg"], s["s"].astype(BF16))
    for s in st:
        h = s["h"]
        o_refs[s["d"]][s["rows"], h * GLA_DV:(h + 1) * GLA_DV] = s["o"].astype(BF16)
        decay = jnp.broadcast_to(s["e_end"], (GLA_DK, GLA_DK)).T
        decay = jnp.concatenate([decay] * (GLA_DV // GLA_DK), axis=1)
        s_ref[s["d"], h] = decay * s["s"] + s["kv"]


def _gla_kernel(qkf, vf, smf, qkb, vb, smb, lrw, lrb, s0, of, ob, sfin, s_ref, g_ref, *, nchunk):
    i = pl.program_id(1)

    @pl.when(i == 0)
    def _():
        s_ref[...] = s0[...]

    _gla_gates(0, smf, lrw, lrb, g_ref)
    _gla_gates(1, smb, lrw, lrb, g_ref)
    local = lambda t: (_gla_local(0, t, qkf, vf, g_ref) + _gla_local(1, nchunk - 1 - t, qkb, vb, g_ref))
    pending = local(0)
    for t in range(nchunk):
        ahead = local(t + 1) if t + 1 < nchunk else None
        _gla_recur(pending, (of, ob), s_ref)
        pending = ahead

    @pl.when(i == pl.num_programs(1) - 1)
    def _():
        sfin[...] = s_ref[...]


def _gla_call(gqk, gv, sm, lrw_pad, lrb, s0, tb):
    b, l, _ = gqk.shape
    nblk = l // tb
    fwd = lambda bi, i: (bi, i, 0)
    bwd = lambda bi, i: (bi, nblk - 1 - i, 0)
    sspec = pl.BlockSpec((None, 2, GLA_HEADS, GLA_DK, GLA_DV), lambda bi, i: (bi, 0, 0, 0, 0))
    wide = GLA_HEADS * GLA_DV
    return pl.pallas_call(
        functools.partial(_gla_kernel, nchunk=tb // CHUNK),
        out_shape=[jax.ShapeDtypeStruct((b, l, wide), BF16),
                   jax.ShapeDtypeStruct((b, l, wide), BF16),
                   jax.ShapeDtypeStruct(s0.shape, F32)],
        grid=(b, nblk),
        in_specs=[pl.BlockSpec((None, tb, 2 * GLA_HEADS * GLA_DK), fwd),
                  pl.BlockSpec((None, tb, wide), fwd),
                  pl.BlockSpec((None, tb, LANES), fwd),
                  pl.BlockSpec((None, tb, 2 * GLA_HEADS * GLA_DK), bwd),
                  pl.BlockSpec((None, tb, wide), bwd),
                  pl.BlockSpec((None, tb, LANES), bwd),
                  _resident(lrw_pad.shape),
                  _resident(lrb.shape),
                  sspec],
        out_specs=[pl.BlockSpec((None, tb, wide), fwd),
                   pl.BlockSpec((None, tb, wide), bwd),
                   sspec],
        scratch_shapes=[pltpu.VMEM((2, GLA_HEADS, GLA_DK, GLA_DV), F32),
                        pltpu.VMEM((2, tb, GLA_HEADS * GLA_DK), F32)],
        compiler_params=_params(2),
        name="gla",
    )(gqk, gv, sm, gqk, gv, sm, lrw_pad, lrb, s0)


def _gdn_conv(x_ref, convw_ref, xp_ref, qkv_ref):
    rows = x_ref.shape[0]
    nq = GDN_HEADS * GDN_D
    width = 3 * nq
    zeros = jnp.zeros((CONV_PAD, width), F32)
    xp_ref[0:CONV_PAD, :] = zeros
    xp_ref[CONV_PAD + rows:CONV_PAD + rows + CONV_PAD, :] = zeros
    xp_ref[CONV_PAD:CONV_PAD + rows, :] = x_ref[...].astype(F32)
    acc = None
    for i in range(CONV_W):
        start = CONV_PAD + i - CONV_W // 2
        term = convw_ref[i:i + 1, :] * xp_ref[start:start + rows, :]
        acc = term if acc is None else acc + term
    xs = _silu(acc)
    for h in range(GDN_HEADS):
        qh = xs[:, h * GDN_D:(h + 1) * GDN_D]
        kh = xs[:, nq + h * GDN_D:nq + (h + 1) * GDN_D]
        qn = qh * (lax.rsqrt(jnp.sum(qh * qh, axis=-1, keepdims=True) + EPS) * GDN_D ** -0.5)
        kn = kh * lax.rsqrt(jnp.sum(kh * kh, axis=-1, keepdims=True) + EPS)
        qkv_ref[:, h * GDN_D:(h + 1) * GDN_D] = qn.astype(BF16)
        qkv_ref[:, nq + h * GDN_D:nq + (h + 1) * GDN_D] = kn.astype(BF16)
    qkv_ref[:, 2 * nq:] = xs[:, 2 * nq:].astype(BF16)


def _gdn_gates(d, sm_ref, alog_ref, dtb_ref, expand_ref, gx_ref, gt_ref, bt_ref):
    rev = d == 1
    rows = sm_ref.shape[0]
    sm = sm_ref[...]
    gate = -jnp.exp(alog_ref[...]) * _softplus(sm + dtb_ref[...])
    tri = jnp.where(_tri(rev), 1.0, 0.0).astype(BF16)
    nchunk = rows // CHUNK
    hi, mid, lo = _split3(gate)
    cum = lambda c: sum(_dot(tri, piece[c * CHUNK:(c + 1) * CHUNK, :]) for piece in (hi, mid, lo))
    gcum = jnp.concatenate([cum(c) for c in range(nchunk)], axis=0)
    onehot = expand_ref[d]
    hi, mid, lo = _split3(gcum)
    gx_ref[d] = _dot(hi, onehot) + _dot(mid, onehot) + _dot(lo, onehot)
    g_t = gcum.T
    b_t = _sigmoid(sm).T
    for c in range(nchunk):
        gt_ref[d, c] = g_t[:, c * CHUNK:(c + 1) * CHUNK]
        bt_ref[d, c] = b_t[:, c * CHUNK:(c + 1) * CHUNK]


def _gdn_local(chains, qkv_refs, gx_ref, gt_ref, bt_ref):
    nq = GDN_HEADS * GDN_D
    row = lax.broadcasted_iota(jnp.int32, (CHUNK, CHUNK), 0)
    col = lax.broadcasted_iota(jnp.int32, (CHUNK, CHUNK), 1)
    eye = row == col
    eye_f = eye.astype(F32)
    st = []
    for d, h, c in chains:
        rev = d == 1
        rs = slice(c * CHUNK, (c + 1) * CHUNK)
        lane_g = SM_A + GDN_HEADS * d + h
        lane_b = SM_B + GDN_HEADS * d + h
        end = c * CHUNK if rev else (c + 1) * CHUNK - 1
        qkv_ref = qkv_refs[d]
        q = qkv_ref[rs, h * GDN_D:(h + 1) * GDN_D]
        k = qkv_ref[rs, nq + h * GDN_D:nq + (h + 1) * GDN_D]
        st.append(dict(
            d=d, h=h, c=c, rs=rs, k=k,
            strict=(col > row) if rev else (col < row),
            qk=jnp.concatenate([q, k], axis=0),
            v=qkv_ref[rs, 2 * nq + h * GDN_D:2 * nq + (h + 1) * GDN_D],
            gj=gt_ref[d, c, lane_g:lane_g + 1, :],
            bj=bt_ref[d, c, lane_b:lane_b + 1, :],
            gi=gx_ref[d, rs, h * GDN_D:h * GDN_D + CHUNK],
            g_end=gx_ref[d, end:end + 1, h * GDN_D:(h + 1) * GDN_D]))
    for s in st:
        s["qkk"] = _dot_nt(s["qk"], s["k"])
    for s in st:
        dec = jnp.exp(jnp.where(s["strict"], s["gi"] - s["gj"], -1e30))
        s["dec"] = dec
        x = -(s["qkk"][CHUNK:] * dec * s["bj"])
        s["x"] = x
        s["p"] = eye_f + jnp.where((row >> 1) == (col >> 1), x, 0.0)
    for level in range(1, 6):
        sibling = ((row >> level) ^ 1) == (col >> level)
        for s in st:
            s["pb"] = s["p"].astype(BF16)
            s["t"] = _dot(jnp.where(sibling, s["x"], 0.0).astype(BF16), s["pb"])
        for s in st:
            s["p"] = s["p"] + _dot(s["pb"], s["t"].astype(BF16))
    for s in st:
        rb = s["p"].astype(BF16)
        s["rv"] = _dot(rb, s["v"])
    for s in st:
        e_gj = jnp.exp(s["gj"])
        s["rg"] = (s["p"] * e_gj).astype(BF16)
        s["e"] = jnp.where(eye, e_gj, 0.0).astype(BF16)
        a_beta = s["qkk"][:CHUNK] * (s["dec"] + eye_f) * s["bj"]
        kscale = jnp.where(eye, jnp.exp(s["g_end"][:, :CHUNK] - s["gj"]) * s["bj"], 0.0)
        s["ak"] = jnp.concatenate([a_beta, kscale], axis=0).astype(BF16)
    return st


def _gdn_recur(st, o_refs, s_ref):
    for s in st:
        s["s"] = s_ref[s["d"], s["h"]]
        s["qks"] = _dot(s["qk"], s["s"].astype(BF16))
    for s in st:
        s["vhat"] = (s["rv"] - _dot(s["rg"], s["qks"][CHUNK:].astype(BF16))).astype(BF16)
    for s in st:
        s["av"] = _dot(s["ak"], s["vhat"])
        s["eq"] = _dot(s["e"], s["qks"][:CHUNK].astype(BF16))
    for s in st:
        h = s["h"]
        o_refs[s["d"]][s["rs"], h * GDN_D:(h + 1) * GDN_D] = (s["eq"] + s["av"][:CHUNK]).astype(BF16)
        s_ref[s["d"], h] = jnp.exp(s["g_end"]) * s["s"] + _dot_tn(s["k"], s["av"][CHUNK:].astype(BF16))


def _gdn_scan(qkv_refs, smf, smb, alog, dtb, expand, s0, of, ob, sfin, s_ref, gx_ref, gt_ref, bt_ref):
    i = pl.program_id(1)

    @pl.when(i == 0)
    def _():
        s_ref[...] = s0[...]

    _gdn_gates(0, smf, alog, dtb, expand, gx_ref, gt_ref, bt_ref)
    _gdn_gates(1, smb, alog, dtb, expand, gx_ref, gt_ref, bt_ref)
    nchunk = smf.shape[0] // CHUNK
    for step in range(nchunk):
        chains = [(0, h, step) for h in range(GDN_HEADS)] + [(1, h, nchunk - 1 - step) for h in range(GDN_HEADS)]
        _gdn_recur(_gdn_local(chains, qkv_refs, gx_ref, gt_ref, bt_ref), (of, ob), s_ref)

    @pl.when(i == pl.num_programs(1) - 1)
    def _():
        sfin[...] = s_ref[...]


def _gdn_kernel(qkvf, smf, qkvb, smb, alog, dtb, expand, s0, of, ob, sfin, s_ref, gx_ref, gt_ref, bt_ref):
    _gdn_scan((qkvf, qkvb), smf, smb, alog, dtb, expand, s0, of, ob, sfin, s_ref, gx_ref, gt_ref, bt_ref)


def _gdn_raw_kernel(xf, smf, xb, smb, convw, alog, dtb, expand, s0, of, ob, sfin,
                    s_ref, gx_ref, gt_ref, bt_ref, xp_ref, qkv_ref):
    _gdn_conv(xf, convw, xp_ref, qkv_ref.at[0])
    _gdn_conv(xb, convw, xp_ref, qkv_ref.at[1])
    _gdn_scan((qkv_ref.at[0], qkv_ref.at[1]), smf, smb, alog, dtb, expand, s0, of, ob, sfin,
              s_ref, gx_ref, gt_ref, bt_ref)


def _gdn_call(qkv, sm, alog_row, dtb_row, expand, s0, conv_w=None):
    b, ncol, rows, width = qkv.shape
    wide = GDN_HEADS * GDN_D
    nchunk = rows // CHUNK
    fwd = lambda bi, i: (bi, i, 0, 0)
    bwd = lambda bi, i: (bi, ncol - 1 - i, 0, 0)
    slab = lambda w, index: pl.BlockSpec((None, None, rows, w), index)
    sspec = pl.BlockSpec((None, 2, GDN_HEADS, GDN_D, GDN_D), lambda bi, i: (bi, 0, 0, 0, 0))
    params = [alog_row, dtb_row, expand]
    scratch = [pltpu.VMEM((2, GDN_HEADS, GDN_D, GDN_D), F32),
               pltpu.VMEM((2, rows, wide), F32),
               pltpu.VMEM((2, nchunk, LANES, CHUNK), F32),
               pltpu.VMEM((2, nchunk, LANES, CHUNK), F32)]
    body = _gdn_kernel
    if conv_w is not None:
        params = [conv_w] + params
        scratch += [pltpu.VMEM((rows + 2 * CONV_PAD, width), F32),
                    pltpu.VMEM((2, rows, width), BF16)]
        body = _gdn_raw_kernel
    return pl.pallas_call(
        body,
        out_shape=[jax.ShapeDtypeStruct((b, ncol, rows, wide), BF16),
                   jax.ShapeDtypeStruct((b, ncol, rows, wide), BF16),
                   jax.ShapeDtypeStruct(s0.shape, F32)],
        grid=(b, ncol),
        in_specs=[slab(width, fwd), slab(LANES, fwd), slab(width, bwd), slab(LANES, bwd)]
                 + [_resident(p.shape) for p in params] + [sspec],
        out_specs=[slab(wide, fwd), slab(wide, bwd), sspec],
        scratch_shapes=scratch,
        compiler_params=_params(2),
        name="gdn",
    )(qkv, sm, qkv, sm, *params, s0)


def _head_norm(o, heads, gain):
    dv = o.shape[-1] // heads
    parts = []
    for h in range(heads):
        oh = o[:, h * dv:(h + 1) * dv]
        parts.append(oh * lax.rsqrt(jnp.mean(oh * oh, axis=-1, keepdims=True) + EPS))
    return jnp.concatenate(parts, axis=-1) * gain


def _tail_kernel(x_ref, glaf, glab, gz, gdnf, gdnb, dz, og, gate1, shift2, scale2, gate2,
                 gla_gain, gdn_gain, n2w, fw, w_out, w_gate, w_up, w_down, out_ref):
    d = x_ref.shape[-1]
    f32 = lambda ref: ref[...].astype(F32)
    y_gla = _head_norm(f32(glaf) + f32(glab), GLA_HEADS, gla_gain[...]) * _silu(f32(gz))
    y_gdn = _head_norm(f32(gdnf) + f32(gdnb), GDN_HEADS, gdn_gain[...]) * _silu(f32(dz))
    gates = f32(og)
    merged = _sigmoid(gates[:, :d]) * y_gla + _sigmoid(gates[:, d:]) * y_gdn
    x1 = x_ref[...] + gate1[...] * _dot(merged.astype(BF16), w_out[...])
    h2 = (_rms(x1, n2w[...]) * (1.0 + scale2[...]) + shift2[...]).astype(BF16)
    hidden = _silu(_dot(h2, w_gate[...])) * _dot(h2, w_up[...])
    x2 = x1 + gate2[...] * _dot(hidden.astype(BF16), w_down[...])
    out_ref[...] = _rms(x2, fw[...])


def _tail_call(x2d, glaf, glab, gz, gdnf, gdnb, dz, og, mod3, gla_gain, gdn_gain, n2w, fw,
               w_out, w_gate, w_up, w_down, tokens_per_mod, tm):
    n, d = x2d.shape
    mod_row = lambda i: (i * tm) // tokens_per_mod
    tok = lambda w: pl.BlockSpec((tm, w), lambda i: (i, 0))
    modspec = lambda j: pl.BlockSpec((None, 1, d), lambda i: (mod_row(i), 0, j))
    return pl.pallas_call(
        _tail_kernel,
        out_shape=jax.ShapeDtypeStruct((n, d), F32),
        grid=(n // tm,),
        in_specs=[tok(d)] * 7 + [tok(2 * d)]
                 + [modspec(2), modspec(3), modspec(4), modspec(5)]
                 + [_resident(a.shape) for a in (gla_gain, gdn_gain, n2w, fw, w_out, w_gate, w_up, w_down)],
        out_specs=tok(d),
        compiler_params=_params(1),
        name="tail",
    )(x2d, glaf, glab, gz, gdnf, gdnb, dz, og, mod3, mod3, mod3, mod3,
      gla_gain, gdn_gain, n2w, fw, w_out, w_gate, w_up, w_down)


def _split_w_in(w_in):
    d = w_in.shape[0]
    gq, gk, gv, gz, lrf, lrb, dq, dk, dv, dz, ab, p11, p12 = jnp.split(
        w_in, np.cumsum([512, 512, 1024, 1024, 16, 16, 1024, 1024, 1024, 1024, 32, 1024]).tolist(), axis=1)
    pad = jnp.zeros((d, LANES - 2 * GLA_LOWRANK - 4 * GDN_HEADS), w_in.dtype)
    cat = lambda *parts: jnp.concatenate(parts, axis=1).astype(BF16)
    return dict(gqk=cat(gq, gk), gv=cat(gv), gz=cat(gz), dqkv=cat(dq, dk, dv), dz=cat(dz),
                og=cat(p11, p12), sm=cat(lrf, lrb, ab, pad))


def _gdn_expand():
    e = np.zeros((2, LANES, GDN_HEADS * GDN_D), np.float32)
    for d in range(2):
        for h in range(GDN_HEADS):
            e[d, SM_A + GDN_HEADS * d + h, h * GDN_D:(h + 1) * GDN_D] = 1.0
    return jnp.asarray(e, dtype=BF16)


def kernel(x, c, ctx, c_ctx, w_mod, b_mod, norm1_w, norm2_w, w_in, gla_lr_w, gla_lr_b, gla_norm_w, gdn_conv_w, gdn_a_log, gdn_dt_bias, gdn_norm_w, w_out, ffn_w_gate, ffn_w_up, ffn_w_down, final_norm_w):
    b, seq, d = x.shape
    ctx_len = ctx.shape[1]
    depth = w_mod.shape[0]
    assert depth == 1, "only the single-layer block is implemented"
    assert d == GLA_HEADS * GLA_DV == GDN_HEADS * GDN_D
    rows = seq // GRID_W
    assert seq % GRID_W == 0 and rows % CHUNK == 0 and ctx_len % CHUNK == 0
    layer = 0

    mod_rows = -(-(b + 1) // 8) * 8
    cc = jnp.zeros((mod_rows, d), F32).at[:b].set(c).at[b].set(c_ctx)
    mod = _mod_call(cc, w_mod[layer], b_mod[layer][None, :])
    mod_l = mod[:b, None, :]
    mod_c = mod[b:b + 1, None, :]

    w = _split_w_in(w_in[layer])
    n1w = norm1_w[layer][None, :]
    conv_w = gdn_conv_w[layer]
    tm = 256
    gqk_l, gv_l, gz_l, dz_l, og_l, sm_l = _inproj_call(
        x.reshape(b * seq, d), mod_l, n1w, [w[k] for k in ("gqk", "gv", "gz", "dz", "og", "sm")], seq, tm)
    gqk_c, gv_c, dqkv_c, sm_c = _inproj_call(
        ctx.reshape(b * ctx_len, d), mod_c, n1w, [w[k] for k in ("gqk", "gv", "dqkv", "sm")], b * ctx_len, tm)
    qkv_l, smg_l = _gproj_call(x.reshape(b, rows, GRID_W, d), mod_l, n1w,
                               jnp.concatenate([w["dqkv"], w["sm"]], axis=1), conv_w)

    lrw_pad = jnp.zeros((2, LANES, GLA_HEADS * GLA_DK), F32)
    lrw_pad = lrw_pad.at[0, SM_LRF:SM_LRF + GLA_LOWRANK].set(gla_lr_w[layer, 0])
    lrw_pad = lrw_pad.at[1, SM_LRB:SM_LRB + GLA_LOWRANK].set(gla_lr_w[layer, 1])
    lrb = gla_lr_b[layer][:, None, :]
    r3 = lambda t, l: t.reshape(b, l, t.shape[-1])
    s0_gla = jnp.zeros((b, 2, GLA_HEADS, GLA_DK, GLA_DV), F32)
    _, _, s_gla = _gla_call(r3(gqk_c, ctx_len), r3(gv_c, ctx_len), r3(sm_c, ctx_len), lrw_pad, lrb, s0_gla, ctx_len)
    gla_f, gla_b, _ = _gla_call(r3(gqk_l, seq), r3(gv_l, seq), r3(sm_l, seq), lrw_pad, lrb, s_gla, 256)

    lane_params = lambda p: jnp.zeros((1, LANES), F32).at[0, SM_A:SM_A + 2 * GDN_HEADS].set(p[layer].reshape(-1))
    alog_row, dtb_row = lane_params(gdn_a_log), lane_params(gdn_dt_bias)
    expand = _gdn_expand()
    s0_gdn = jnp.zeros((b, 2, GDN_HEADS, GDN_D, GDN_D), F32)
    one_col = lambda t: t.reshape(b, 1, ctx_len, t.shape[-1])
    _, _, s_gdn = _gdn_call(one_col(dqkv_c), one_col(sm_c), alog_row, dtb_row, expand, s0_gdn, conv_w=conv_w)
    gdn_f, gdn_b, _ = _gdn_call(qkv_l, smg_l, alog_row, dtb_row, expand, s_gdn)

    flat = lambda t: t.reshape(b * seq, d)
    raster = lambda t: t.transpose(0, 2, 1, 3).reshape(b * seq, d)
    out = _tail_call(
        x.reshape(b * seq, d), flat(gla_f), flat(gla_b), gz_l, raster(gdn_f), raster(gdn_b), dz_l, og_l, mod_l,
        gla_norm_w[layer].reshape(1, d), gdn_norm_w[layer].reshape(1, d),
        norm2_w[layer][None, :], final_norm_w[None, :],
        w_out[layer].astype(BF16), ffn_w_gate[layer].astype(BF16), ffn_w_up[layer].astype(BF16),
        ffn_w_down[layer].astype(BF16), seq, tm)
    return out.reshape(b, seq, d)
```

```python
import functools

import numpy as np
import jax
import jax.numpy as jnp
from jax import lax
from jax.experimental import pallas as pl
from jax.experimental.pallas import tpu as pltpu

F32 = jnp.float32
BF16 = jnp.bfloat16
HI = lax.Precision.HIGHEST

EPS = 1e-6
CHUNK = 64
GRID_W = 64
CONV_W = 5
CONV_PAD = 8

GLA_HEADS = 4
GLA_DK = 128
GLA_DV = 256
GLA_LOWRANK = 16
GLA_GATE_NORM = 16.0
GDN_HEADS = 8
GDN_D = 128
GDN_GROUP = 4
WIDE = GDN_GROUP * CHUNK
GDN_STEP_COLS = 4

LANES = 128
VMEM_LIMIT = 56 * 1024 * 1024

SM_LRF, SM_LRB, SM_A, SM_B = 0, 16, 32, 48


def _dot(a, b, **kw):
    return jnp.dot(a, b, preferred_element_type=F32, **kw)


def _dot_nt(a, b):
    return lax.dot_general(a, b, (((1,), (1,)), ((), ())), preferred_element_type=F32)


def _dot_tn(a, b):
    return lax.dot_general(a, b, (((0,), (0,)), ((), ())), preferred_element_type=F32)


def _sigmoid(t):
    return 1.0 / (1.0 + jnp.exp(-t))


def _silu(t):
    return t * _sigmoid(t)


def _softplus(t):
    return jnp.maximum(t, 0.0) + jnp.log(1.0 + jnp.exp(-jnp.abs(t)))


def _rms(t, w):
    return t * lax.rsqrt(jnp.mean(t * t, axis=-1, keepdims=True) + EPS) * w


def _params(n_grid):
    return pltpu.CompilerParams(dimension_semantics=("arbitrary",) * n_grid,
                                vmem_limit_bytes=VMEM_LIMIT)


def _resident(shape):
    return pl.BlockSpec(shape, lambda *_: (0,) * len(shape), pipeline_mode=pl.Buffered(1))


def _mod_kernel(c_ref, w_ref, b_ref, o_ref):
    o_ref[...] = _dot(_silu(c_ref[...]), w_ref[...], precision=HI) + b_ref[...]


def _mod_call(cc, w_mod, b_mod):
    rows, d = cc.shape
    n = w_mod.shape[1]
    return pl.pallas_call(
        _mod_kernel,
        out_shape=jax.ShapeDtypeStruct((rows, n), F32),
        grid=(n // d,),
        in_specs=[pl.BlockSpec((rows, d), lambda j: (0, 0)),
                  pl.BlockSpec((d, d), lambda j: (0, j)),
                  pl.BlockSpec((1, d), lambda j: (0, j))],
        out_specs=pl.BlockSpec((rows, d), lambda j: (0, j)),
        compiler_params=_params(1),
        name="mod",
    )(cc, w_mod, b_mod)


def _inproj_kernel(x_ref, shift_ref, scale_ref, nw_ref, w_ref, *out_refs):
    h = _rms(x_ref[...], nw_ref[...]) * (1.0 + scale_ref[...]) + shift_ref[...]
    hb = h.astype(BF16)
    lo = 0
    for ref in out_refs:
        width = ref.shape[-1]
        ref[...] = _dot(hb, w_ref[:, lo:lo + width]).astype(ref.dtype)
        lo += width


def _inproj_call(x2d, mod3, norm_w, weights, tokens_per_mod, tm):
    n, d = x2d.shape
    widths = [w.shape[1] for w in weights]
    w_all = jnp.concatenate(weights, axis=1)
    mod_row = lambda i: (i * tm) // tokens_per_mod
    return pl.pallas_call(
        _inproj_kernel,
        out_shape=[jax.ShapeDtypeStruct((n, w), F32 if w == LANES else BF16) for w in widths],
        grid=(n // tm,),
        in_specs=[pl.BlockSpec((tm, d), lambda i: (i, 0)),
                  pl.BlockSpec((None, 1, d), lambda i: (mod_row(i), 0, 0)),
                  pl.BlockSpec((None, 1, d), lambda i: (mod_row(i), 0, 1)),
                  _resident((1, d)),
                  _resident(w_all.shape)],
        out_specs=[pl.BlockSpec((tm, w), lambda i: (i, 0)) for w in widths],
        compiler_params=_params(1),
        name="inproj",
    )(x2d, mod3, mod3, norm_w, w_all)


GPROJ_COLS = 8
GPROJ_WIDTH = 2 * LANES


def _gproj_kernel(x_ref, shift_ref, scale_ref, nw_ref, w_ref, convw_ref, qkv_ref, smg_ref,
                  pad_ref, y_ref, *, rows):
    nq = GDN_HEADS * GDN_D
    n = rows * GPROJ_COLS
    halo = (CONV_W // 2) * GPROJ_COLS
    x = x_ref[...].reshape(n, x_ref.shape[-1])
    hb = (_rms(x, nw_ref[...]) * (1.0 + scale_ref[...]) + shift_ref[...]).astype(BF16)
    zeros = jnp.zeros((halo, GPROJ_WIDTH), F32)
    pad_ref[0:halo, :] = zeros
    pad_ref[halo + n:halo + n + halo, :] = zeros
    column = lambda slab, c: y_ref[slab, pl.ds(c, rows, stride=GPROJ_COLS), :]
    for lo in range(0, 3 * nq, GPROJ_WIDTH):
        pad_ref[halo:halo + n, :] = _dot(hb, w_ref[:, lo:lo + GPROJ_WIDTH])
        acc = None
        for i in range(CONV_W):
            start = halo + (i - CONV_W // 2) * GPROJ_COLS
            term = convw_ref[i:i + 1, lo:lo + GPROJ_WIDTH] * pad_ref[start:start + n, :]
            acc = term if acc is None else acc + term
        y = _silu(acc)
        for slab in range(GPROJ_WIDTH // LANES):
            yh = y[:, slab * LANES:(slab + 1) * LANES]
            if lo < 2 * nq:
                scale = GDN_D ** -0.5 if lo < nq else 1.0
                yh = yh * (lax.rsqrt(jnp.sum(yh * yh, axis=-1, keepdims=True) + EPS) * scale)
            y_ref[slab] = yh
        for c in range(GPROJ_COLS):
            for slab in range(GPROJ_WIDTH // LANES):
                qkv_ref[c, :, lo + slab * LANES:lo + (slab + 1) * LANES] = column(slab, c).astype(BF16)
    y_ref[0] = _dot(hb, w_ref[:, 3 * nq:3 * nq + LANES])
    for c in range(GPROJ_COLS):
        smg_ref[c] = column(0, c)


def _gproj_call(x4, mod3, norm_w, w_g, conv_w):
    b, rows, ncol, d = x4.shape
    width = 3 * GDN_HEADS * GDN_D
    assert GDN_D == LANES and ncol % GPROJ_COLS == 0
    n = rows * GPROJ_COLS
    halo = (CONV_W // 2) * GPROJ_COLS
    return pl.pallas_call(
        functools.partial(_gproj_kernel, rows=rows),
        out_shape=[jax.ShapeDtypeStruct((b, ncol, rows, width), BF16),
                   jax.ShapeDtypeStruct((b, ncol, rows, LANES), F32)],
        grid=(b, ncol // GPROJ_COLS),
        in_specs=[pl.BlockSpec((None, rows, GPROJ_COLS, d), lambda bi, j: (bi, 0, j, 0)),
                  pl.BlockSpec((None, 1, d), lambda bi, j: (bi, 0, 0)),
                  pl.BlockSpec((None, 1, d), lambda bi, j: (bi, 0, 1)),
                  _resident((1, d)),
                  _resident(w_g.shape),
                  _resident(conv_w.shape)],
        out_specs=[pl.BlockSpec((None, GPROJ_COLS, rows, width), lambda bi, j: (bi, j, 0, 0)),
                   pl.BlockSpec((None, GPROJ_COLS, rows, LANES), lambda bi, j: (bi, j, 0, 0))],
        scratch_shapes=[pltpu.VMEM((n + 2 * halo, GPROJ_WIDTH), F32),
                        pltpu.VMEM((GPROJ_WIDTH // LANES, n, LANES), F32)],
        compiler_params=_params(2),
        name="gproj",
    )(x4, mod3, mod3, norm_w, w_g, conv_w)


def _tri(rev):
    row = lax.broadcasted_iota(jnp.int32, (CHUNK, CHUNK), 0)
    col = lax.broadcasted_iota(jnp.int32, (CHUNK, CHUNK), 1)
    return (col >= row) if rev else (col <= row)


def _split3(t):
    hi = t.astype(BF16)
    r1 = t - hi.astype(F32)
    mid = r1.astype(BF16)
    lo = (r1 - mid.astype(F32)).astype(BF16)
    return hi, mid, lo


def _dot3(a, b):
    a1 = a.astype(BF16)
    a2 = (a - a1.astype(F32)).astype(BF16)
    b1 = b.astype(BF16)
    b2 = (b - b1.astype(F32)).astype(BF16)
    return _dot(a1, b1) + _dot(a1, b2) + _dot(a2, b1)


def _gla_gates(d, sm_ref, lrw_ref, lrb_ref, g_ref):
    tb = sm_ref.shape[0]
    gpre = _dot3(sm_ref[...], lrw_ref[d]) + lrb_ref[d]
    g = (jnp.minimum(gpre, 0.0) - jnp.log(1.0 + jnp.exp(-jnp.abs(gpre)))) * (1.0 / GLA_GATE_NORM)
    row = lax.broadcasted_iota(jnp.int32, (tb, tb), 0)
    col = lax.broadcasted_iota(jnp.int32, (tb, tb), 1)
    order = (col >= row) if d == 1 else (col <= row)
    same_chunk = (row >> 6) == (col >> 6)
    tri = jnp.where(same_chunk, jnp.where(order, 1.0, 0.0), 0.0).astype(BF16)
    hi, mid, lo = _split3(g)
    g_ref[d] = _dot(tri, hi) + _dot(tri, mid) + _dot(tri, lo)


def _gla_local(d, c, qk_ref, v_ref, g_ref):
    rev = d == 1
    incl = _tri(rev)
    rows = slice(c * CHUNK, (c + 1) * CHUNK)
    qk = qk_ref[rows, :].astype(F32)
    v = v_ref[rows, :]
    G = g_ref[d, rows, :]
    g_mid = G[CHUNK // 2:CHUNK // 2 + 1, :]
    g_end = G[0:1, :] if rev else G[CHUNK - 1:CHUNK, :]
    e_g = jnp.exp(G)
    e_q = jnp.exp(G - g_mid)
    e_k = jnp.exp(g_mid - G)
    e_kend = jnp.exp(g_end - G)
    e_end = jnp.exp(g_end)
    nqk = GLA_HEADS * GLA_DK
    st = []
    for h in range(GLA_HEADS):
        sl = slice(h * GLA_DK, (h + 1) * GLA_DK)
        q = qk[:, sl] * (GLA_DK ** -0.5)
        k = qk[:, nqk + h * GLA_DK:nqk + (h + 1) * GLA_DK]
        st.append(dict(d=d, h=h, rows=rows, incl=incl,
                       v=v[:, h * GLA_DV:(h + 1) * GLA_DV],
                       qt=(q * e_q[:, sl]).astype(BF16), kt=(k * e_k[:, sl]).astype(BF16),
                       qg=(q * e_g[:, sl]).astype(BF16), ke=(k * e_kend[:, sl]).astype(BF16),
                       e_end=e_end[:, sl]))
    for s in st:
        s["a"] = jnp.where(s["incl"], _dot_nt(s["qt"], s["kt"]), 0.0).astype(BF16)
    for s in st:
        s["kv"] = _dot_tn(s["ke"], s["v"])
    return st


def _gla_recur(st, o_refs, s_ref):
    for s in st:
        s["s"] = s_ref[s["d"], s["h"]]
        s["o"] = _dot(s["a"], s["v"]) + _dot(s["qg"], s["s"].astype(BF16))
    for s in st:
        h = s["h"]
        o_refs[s["d"]][s["rows"], h * GLA_DV:(h + 1) * GLA_DV] = s["o"].astype(BF16)
        decay = jnp.broadcast_to(s["e_end"], (GLA_DK, GLA_DK)).T
        decay = jnp.concatenate([decay] * (GLA_DV // GLA_DK), axis=1)
        s_ref[s["d"], h] = decay * s["s"] + s["kv"]


def _gla_kernel(qkf, vf, smf, qkb, vb, smb, lrw, lrb, s0, of, ob, sfin, s_ref, g_ref, *, nchunk):
    i = pl.program_id(1)

    @pl.when(i == 0)
    def _():
        s_ref[...] = s0[...]

    _gla_gates(0, smf, lrw, lrb, g_ref)
    _gla_gates(1, smb, lrw, lrb, g_ref)
    local = lambda t: (_gla_local(0, t, qkf, vf, g_ref) + _gla_local(1, nchunk - 1 - t, qkb, vb, g_ref))
    pending = local(0)
    for t in range(nchunk):
        ahead = local(t + 1) if t + 1 < nchunk else None
        _gla_recur(pending, (of, ob), s_ref)
        pending = ahead

    @pl.when(i == pl.num_programs(1) - 1)
    def _():
        sfin[...] = s_ref[...]


def _gla_call(gqk, gv, sm, lrw_pad, lrb, s0, tb):
    b, l, _ = gqk.shape
    nblk = l // tb
    fwd = lambda bi, i: (bi, i, 0)
    bwd = lambda bi, i: (bi, nblk - 1 - i, 0)
    sspec = pl.BlockSpec((None, 2, GLA_HEADS, GLA_DK, GLA_DV), lambda bi, i: (bi, 0, 0, 0, 0))
    wide = GLA_HEADS * GLA_DV
    return pl.pallas_call(
        functools.partial(_gla_kernel, nchunk=tb // CHUNK),
        out_shape=[jax.ShapeDtypeStruct((b, l, wide), BF16),
                   jax.ShapeDtypeStruct((b, l, wide), BF16),
                   jax.ShapeDtypeStruct(s0.shape, F32)],
        grid=(b, nblk),
        in_specs=[pl.BlockSpec((None, tb, 2 * GLA_HEADS * GLA_DK), fwd),
                  pl.BlockSpec((None, tb, wide), fwd),
                  pl.BlockSpec((None, tb, LANES), fwd),
                  pl.BlockSpec((None, tb, 2 * GLA_HEADS * GLA_DK), bwd),
                  pl.BlockSpec((None, tb, wide), bwd),
                  pl.BlockSpec((None, tb, LANES), bwd),
                  _resident(lrw_pad.shape),
                  _resident(lrb.shape),
                  sspec],
        out_specs=[pl.BlockSpec((None, tb, wide), fwd),
                   pl.BlockSpec((None, tb, wide), bwd),
                   sspec],
        scratch_shapes=[pltpu.VMEM((2, GLA_HEADS, GLA_DK, GLA_DV), F32),
                        pltpu.VMEM((2, tb, GLA_HEADS * GLA_DK), F32)],
        compiler_params=_params(2),
        name="gla",
    )(gqk, gv, sm, gqk, gv, sm, lrw_pad, lrb, s0)


def _gdn_conv(x_ref, convw_ref, xp_ref, qkv_ref):
    rows = x_ref.shape[0]
    nq = GDN_HEADS * GDN_D
    width = 3 * nq
    zeros = jnp.zeros((CONV_PAD, width), F32)
    xp_ref[0:CONV_PAD, :] = zeros
    xp_ref[CONV_PAD + rows:CONV_PAD + rows + CONV_PAD, :] = zeros
    xp_ref[CONV_PAD:CONV_PAD + rows, :] = x_ref[...].astype(F32)
    acc = None
    for i in range(CONV_W):
        start = CONV_PAD + i - CONV_W // 2
        term = convw_ref[i:i + 1, :] * xp_ref[start:start + rows, :]
        acc = term if acc is None else acc + term
    xs = _silu(acc)
    for h in range(GDN_HEADS):
        qh = xs[:, h * GDN_D:(h + 1) * GDN_D]
        kh = xs[:, nq + h * GDN_D:nq + (h + 1) * GDN_D]
        qn = qh * (lax.rsqrt(jnp.sum(qh * qh, axis=-1, keepdims=True) + EPS) * GDN_D ** -0.5)
        kn = kh * lax.rsqrt(jnp.sum(kh * kh, axis=-1, keepdims=True) + EPS)
        qkv_ref[:, h * GDN_D:(h + 1) * GDN_D] = qn.astype(BF16)
        qkv_ref[:, nq + h * GDN_D:nq + (h + 1) * GDN_D] = kn.astype(BF16)
    qkv_ref[:, 2 * nq:] = xs[:, 2 * nq:].astype(BF16)


def _dot_exact(t, onehot):
    hi, mid, lo = _split3(t)
    return _dot(hi, onehot) + _dot(mid, onehot) + _dot(lo, onehot)


def _chunk_end(rev, c):
    return c * CHUNK if rev else (c + 1) * CHUNK - 1


def _gdn_gates(d, sm_ref, alog_ref, dtb_ref, e64_ref, e128_ref, gi_ref, gt_ref, bt_ref, ge_ref):
    rev = d == 1
    rows = sm_ref.shape[0]
    sm = sm_ref[...]
    gate = -jnp.exp(alog_ref[...]) * _softplus(sm + dtb_ref[...])
    tri = jnp.where(_tri(rev), 1.0, 0.0).astype(BF16)
    nchunk = rows // CHUNK
    gcum = jnp.concatenate([_dot_exact_lhs(tri, gate[c * CHUNK:(c + 1) * CHUNK, :]) for c in range(nchunk)],
                           axis=0)
    gi_ref[d] = _dot_exact(gcum, e64_ref[d])
    gt_ref[d] = gcum.T
    bt_ref[d] = _sigmoid(sm).T
    ends = [gcum[_chunk_end(rev, c):_chunk_end(rev, c) + 1, :] for c in range(nchunk)]
    if nchunk < ge_ref.shape[1]:
        ends.append(jnp.zeros((ge_ref.shape[1] - nchunk, LANES), F32))
    ge_ref[d] = _dot_exact(jnp.concatenate(ends, axis=0), e128_ref[d])


def _dot_exact_lhs(onehot, t):
    hi, mid, lo = _split3(t)
    return _dot(onehot, hi) + _dot(onehot, mid) + _dot(onehot, lo)


def _bd_wide(m):
    low = lax.broadcasted_iota(jnp.int32, (CHUNK, LANES), 1) < CHUNK
    zero = jnp.zeros((CHUNK, LANES), m.dtype)
    rows = []
    for a in range(GDN_GROUP):
        tile = m[:, (a // 2) * LANES:(a // 2 + 1) * LANES]
        kept = jnp.where(low, tile, zero) if a % 2 == 0 else jnp.where(low, zero, tile)
        rows.append(jnp.concatenate([kept, zero] if a < 2 else [zero, kept], axis=1))
    return jnp.concatenate(rows, axis=0)


def _bd_heads(t):
    zero = jnp.zeros((CHUNK, GDN_D), t.dtype)
    return jnp.concatenate(
        [jnp.concatenate([t[:, a * GDN_D:(a + 1) * GDN_D] if b == a else zero for b in range(GDN_GROUP)], axis=1)
         for a in range(GDN_GROUP)], axis=0)


def _gdn_local(groups, qkv_refs, gi_ref, gt_ref, bt_ref, ge_ref):
    nq = GDN_HEADS * GDN_D
    span = GDN_GROUP * GDN_D
    row = lax.broadcasted_iota(jnp.int32, (CHUNK, WIDE), 0)
    col = lax.broadcasted_iota(jnp.int32, (CHUNK, WIDE), 1) & (CHUNK - 1)
    eye = row == col
    eye_f = jnp.where(eye, 1.0, 0.0)

    def row_of(t_ref, base, d, hg, rs):
        lane = lambda a: base + GDN_HEADS * d + GDN_GROUP * hg + a
        return jnp.concatenate([t_ref[d, lane(a):lane(a) + 1, rs] for a in range(GDN_GROUP)], axis=1)

    st = []
    for d, c, hg in groups:
        rev = d == 1
        rs = slice(c * CHUNK, (c + 1) * CHUNK)
        end = _chunk_end(rev, c)
        qkv_ref = qkv_refs[d]
        wl = slice(hg * WIDE, (hg + 1) * WIDE)
        st.append(dict(
            d=d, hg=hg, rs=rs,
            strict=(col > row) if rev else (col < row),
            q=qkv_ref[rs, hg * span:(hg + 1) * span],
            k=qkv_ref[rs, nq + hg * span:nq + (hg + 1) * span],
            v=qkv_ref[rs, 2 * nq + hg * span:2 * nq + (hg + 1) * span],
            gi=gi_ref[d, rs, wl],
            gj=row_of(gt_ref, SM_A, d, hg, rs),
            bj=row_of(bt_ref, SM_B, d, hg, rs),
            g_end=gi_ref[d, end:end + 1, wl],
            s_decay=ge_ref[d, c:c + 1, hg * span:(hg + 1) * span]))
    for s in st:
        s["bdk"] = _bd_heads(s["k"])
        s["qkk"] = _dot_nt(jnp.concatenate([s["q"], s["k"]], axis=0), s["bdk"])
    for s in st:
        dec = jnp.exp(jnp.where(s["strict"], s["gi"] - s["gj"], -1e30))
        s["dec"] = dec
        x = -(s["qkk"][CHUNK:] * dec * s["bj"])
        s["x"] = x
        s["p"] = eye_f + jnp.where((row >> 1) == (col >> 1), x, 0.0)
    for level in range(1, 6):
        sibling = ((row >> level) ^ 1) == (col >> level)
        for s in st:
            s["pb"] = s["p"].astype(BF16)
            s["t"] = _dot(jnp.where(sibling, s["x"], 0.0).astype(BF16), _bd_wide(s["pb"]))
        for s in st:
            s["p"] = s["p"] + _dot(s["pb"], _bd_wide(s["t"].astype(BF16)))
    for s in st:
        e_gj = jnp.exp(s["gj"])
        s["rv"] = _dot(s["p"].astype(BF16), _bd_heads(s["v"]))
        s["w"] = _dot((s["p"] * e_gj).astype(BF16), s["bdk"]).astype(BF16)
        s["e"] = jnp.where(eye, e_gj, 0.0).astype(BF16)
        a_beta = s["qkk"][:CHUNK] * (s["dec"] + eye_f) * s["bj"]
        kscale = jnp.where(eye, jnp.exp(s["g_end"] - s["gj"]) * s["bj"], 0.0)
        s["ak"] = jnp.concatenate([a_beta, kscale], axis=0).astype(BF16)
    return st


def _gdn_recur(st, o_refs, s_ref):
    pair = 2 * GDN_D
    zero = jnp.zeros((GDN_D, GDN_D), BF16)
    for s in st:
        s["s"] = s_ref[s["d"], s["hg"]]
        sb = s["s"].astype(BF16)
        out = []
        for p in range(GDN_GROUP // 2):
            lanes = slice(p * pair, (p + 1) * pair)
            s_a, s_b = sb[:, p * pair:p * pair + GDN_D], sb[:, p * pair + GDN_D:(p + 1) * pair]
            bd = jnp.concatenate([jnp.concatenate([s_a, zero], axis=1),
                                  jnp.concatenate([zero, s_b], axis=1)], axis=0)
            out.append(_dot(jnp.concatenate([s["w"][:, lanes], s["q"][:, lanes]], axis=0), bd))
        s["wqs"] = jnp.concatenate(out, axis=1)
    for s in st:
        vhat = (s["rv"] - s["wqs"][:CHUNK]).astype(BF16)
        s["av"] = _dot(s["ak"], _bd_heads(vhat))
        s["eq"] = _dot(s["e"], _bd_heads(s["wqs"][CHUNK:].astype(BF16)))
    for s in st:
        span = GDN_GROUP * GDN_D
        o_refs[s["d"]][s["rs"], s["hg"] * span:(s["hg"] + 1) * span] = (s["eq"] + s["av"][:CHUNK]).astype(BF16)
        k_rows = jnp.concatenate([s["k"][:, a * GDN_D:(a + 1) * GDN_D] for a in range(GDN_GROUP)], axis=0)
        update = _dot_tn(k_rows, _bd_heads(s["av"][CHUNK:].astype(BF16)))
        s_ref[s["d"], s["hg"]] = jnp.exp(s["s_decay"]) * s["s"] + update


def _gdn_scan(qkv_refs, smf, smb, alog, dtb, e64, e128, s0, of, ob, sfin, s_ref, gi_ref, gt_ref, bt_ref, ge_ref):
    i = pl.program_id(1)

    @pl.when(i == 0)
    def _():
        s_ref[...] = s0[...]

    _gdn_gates(0, smf, alog, dtb, e64, e128, gi_ref, gt_ref, bt_ref, ge_ref)
    _gdn_gates(1, smb, alog, dtb, e64, e128, gi_ref, gt_ref, bt_ref, ge_ref)
    nchunk = smf.shape[0] // CHUNK
    ngroup = GDN_HEADS // GDN_GROUP
    step_groups = lambda t: ([(0, t, hg) for hg in range(ngroup)]
                             + [(1, nchunk - 1 - t, hg) for hg in range(ngroup)])
    local = _gdn_local([g for t in range(nchunk) for g in step_groups(t)], qkv_refs, gi_ref, gt_ref, bt_ref, ge_ref)
    per_step = 2 * ngroup
    for t in range(nchunk):
        _gdn_recur(local[t * per_step:(t + 1) * per_step], (of, ob), s_ref)

    @pl.when(i == pl.num_programs(1) - 1)
    def _():
        sfin[...] = s_ref[...]


def _gdn_kernel(qkvf, smf, qkvb, smb, alog, dtb, e64, e128, s0, of, ob, sfin, s_ref, gi_ref, gt_ref, bt_ref, ge_ref):
    _gdn_scan((qkvf, qkvb), smf, smb, alog, dtb, e64, e128, s0, of, ob, sfin, s_ref, gi_ref, gt_ref, bt_ref, ge_ref)


def _gdn_raw_kernel(xf, smf, xb, smb, convw, alog, dtb, e64, e128, s0, of, ob, sfin,
                    s_ref, gi_ref, gt_ref, bt_ref, ge_ref, xp_ref, qkv_ref):
    _gdn_conv(xf, convw, xp_ref, qkv_ref.at[0])
    _gdn_conv(xb, convw, xp_ref, qkv_ref.at[1])
    _gdn_scan((qkv_ref.at[0], qkv_ref.at[1]), smf, smb, alog, dtb, e64, e128, s0, of, ob, sfin,
              s_ref, gi_ref, gt_ref, bt_ref, ge_ref)


def _gdn_state_shape(b):
    return (b, 2, GDN_HEADS // GDN_GROUP, GDN_D, GDN_GROUP * GDN_D)


def _gdn_call(qkv, sm, alog_row, dtb_row, e64, e128, s0, conv_w=None):
    b, ncol, rows, width = qkv.shape
    wide = GDN_HEADS * GDN_D
    nchunk = rows // CHUNK
    fwd = lambda bi, i: (bi, i, 0, 0)
    bwd = lambda bi, i: (bi, ncol - 1 - i, 0, 0)
    slab = lambda w, index: pl.BlockSpec((None, None, rows, w), index)
    sspec = pl.BlockSpec((None,) + s0.shape[1:], lambda bi, i: (bi, 0, 0, 0, 0))
    params = [alog_row, dtb_row, e64, e128]
    scratch = [pltpu.VMEM(s0.shape[1:], F32),
               pltpu.VMEM((2, rows, GDN_HEADS * CHUNK), F32),
               pltpu.VMEM((2, LANES, rows), F32),
               pltpu.VMEM((2, LANES, rows), F32),
               pltpu.VMEM((2, 8, wide), F32)]
    assert nchunk <= 8
    body = _gdn_kernel
    if conv_w is not None:
        params = [conv_w] + params
        scratch += [pltpu.VMEM((rows + 2 * CONV_PAD, width), F32),
                    pltpu.VMEM((2, rows, width), BF16)]
        body = _gdn_raw_kernel
    return pl.pallas_call(
        body,
        out_shape=[jax.ShapeDtypeStruct((b, ncol, rows, wide), BF16),
                   jax.ShapeDtypeStruct((b, ncol, rows, wide), BF16),
                   jax.ShapeDtypeStruct(s0.shape, F32)],
        grid=(b, ncol),
        in_specs=[slab(width, fwd), slab(LANES, fwd), slab(width, bwd), slab(LANES, bwd)]
                 + [_resident(p.shape) for p in params] + [sspec],
        out_specs=[slab(wide, fwd), slab(wide, bwd), sspec],
        scratch_shapes=scratch,
        compiler_params=_params(2),
        name="gdn",
    )(qkv, sm, qkv, sm, *params, s0)


def _head_norm(o, heads, gain):
    dv = o.shape[-1] // heads
    parts = []
    for h in range(heads):
        oh = o[:, h * dv:(h + 1) * dv]
        parts.append(oh * lax.rsqrt(jnp.mean(oh * oh, axis=-1, keepdims=True) + EPS))
    return jnp.concatenate(parts, axis=-1) * gain


def _tail_kernel(x_ref, glaf, glab, gz, gdnf, gdnb, dz, og, gate1, shift2, scale2, gate2,
                 gla_gain, gdn_gain, n2w, fw, w_out, w_gate, w_up, w_down, out_ref):
    d = x_ref.shape[-1]
    f32 = lambda ref: ref[...].astype(F32)
    y_gla = _head_norm(f32(glaf) + f32(glab), GLA_HEADS, gla_gain[...]) * _silu(f32(gz))
    y_gdn = _head_norm(f32(gdnf) + f32(gdnb), GDN_HEADS, gdn_gain[...]) * _silu(f32(dz))
    gates = f32(og)
    merged = _sigmoid(gates[:, :d]) * y_gla + _sigmoid(gates[:, d:]) * y_gdn
    x1 = x_ref[...] + gate1[...] * _dot(merged.astype(BF16), w_out[...])
    h2 = (_rms(x1, n2w[...]) * (1.0 + scale2[...]) + shift2[...]).astype(BF16)
    hidden = _silu(_dot(h2, w_gate[...])) * _dot(h2, w_up[...])
    x2 = x1 + gate2[...] * _dot(hidden.astype(BF16), w_down[...])
    out_ref[...] = _rms(x2, fw[...])


def _tail_call(x2d, glaf, glab, gz, gdnf, gdnb, dz, og, mod3, gla_gain, gdn_gain, n2w, fw,
               w_out, w_gate, w_up, w_down, tokens_per_mod, tm):
    n, d = x2d.shape
    mod_row = lambda i: (i * tm) // tokens_per_mod
    tok = lambda w: pl.BlockSpec((tm, w), lambda i: (i, 0))
    modspec = lambda j: pl.BlockSpec((None, 1, d), lambda i: (mod_row(i), 0, j))
    return pl.pallas_call(
        _tail_kernel,
        out_shape=jax.ShapeDtypeStruct((n, d), F32),
        grid=(n // tm,),
        in_specs=[tok(d)] * 7 + [tok(2 * d)]
                 + [modspec(2), modspec(3), modspec(4), modspec(5)]
                 + [_resident(a.shape) for a in (gla_gain, gdn_gain, n2w, fw, w_out, w_gate, w_up, w_down)],
        out_specs=tok(d),
        compiler_params=_params(1),
        name="tail",
    )(x2d, glaf, glab, gz, gdnf, gdnb, dz, og, mod3, mod3, mod3, mod3,
      gla_gain, gdn_gain, n2w, fw, w_out, w_gate, w_up, w_down)


def _split_w_in(w_in):
    d = w_in.shape[0]
    gq, gk, gv, gz, lrf, lrb, dq, dk, dv, dz, ab, p11, p12 = jnp.split(
        w_in, np.cumsum([512, 512, 1024, 1024, 16, 16, 1024, 1024, 1024, 1024, 32, 1024]).tolist(), axis=1)
    pad = jnp.zeros((d, LANES - 2 * GLA_LOWRANK - 4 * GDN_HEADS), w_in.dtype)
    cat = lambda *parts: jnp.concatenate(parts, axis=1).astype(BF16)
    return dict(gqk=cat(gq, gk), gv=cat(gv), gz=cat(gz), dqkv=cat(dq, dk, dv), dz=cat(dz),
                og=cat(p11, p12), sm=cat(lrf, lrb, ab, pad))


def _gdn_expand():
    e64 = np.zeros((2, LANES, GDN_HEADS * CHUNK), np.float32)
    e128 = np.zeros((2, LANES, GDN_HEADS * GDN_D), np.float32)
    for d in range(2):
        for h in range(GDN_HEADS):
            e64[d, SM_A + GDN_HEADS * d + h, h * CHUNK:(h + 1) * CHUNK] = 1.0
            e128[d, SM_A + GDN_HEADS * d + h, h * GDN_D:(h + 1) * GDN_D] = 1.0
    return jnp.asarray(e64, dtype=BF16), jnp.asarray(e128, dtype=BF16)


def kernel(x, c, ctx, c_ctx, w_mod, b_mod, norm1_w, norm2_w, w_in, gla_lr_w, gla_lr_b, gla_norm_w, gdn_conv_w, gdn_a_log, gdn_dt_bias, gdn_norm_w, w_out, ffn_w_gate, ffn_w_up, ffn_w_down, final_norm_w):
    b, seq, d = x.shape
    ctx_len = ctx.shape[1]
    depth = w_mod.shape[0]
    assert depth == 1, "only the single-layer block is implemented"
    assert d == GLA_HEADS * GLA_DV == GDN_HEADS * GDN_D
    rows = seq // GRID_W
    assert seq % GRID_W == 0 and rows % CHUNK == 0 and ctx_len % CHUNK == 0
    layer = 0

    mod_rows = -(-(b + 1) // 8) * 8
    cc = jnp.zeros((mod_rows, d), F32).at[:b].set(c).at[b].set(c_ctx)
    mod = _mod_call(cc, w_mod[layer], b_mod[layer][None, :])
    mod_l = mod[:b, None, :]
    mod_c = mod[b:b + 1, None, :]

    w = _split_w_in(w_in[layer])
    n1w = norm1_w[layer][None, :]
    conv_w = gdn_conv_w[layer]
    tm = 256
    gqk_l, gv_l, gz_l, dz_l, og_l, sm_l = _inproj_call(
        x.reshape(b * seq, d), mod_l, n1w, [w[k] for k in ("gqk", "gv", "gz", "dz", "og", "sm")], seq, tm)
    gqk_c, gv_c, dqkv_c, sm_c = _inproj_call(
        ctx.reshape(b * ctx_len, d), mod_c, n1w, [w[k] for k in ("gqk", "gv", "dqkv", "sm")], b * ctx_len, tm)
    qkv_l, smg_l = _gproj_call(x.reshape(b, rows, GRID_W, d), mod_l, n1w,
                               jnp.concatenate([w["dqkv"], w["sm"]], axis=1), conv_w)

    lrw_pad = jnp.zeros((2, LANES, GLA_HEADS * GLA_DK), F32)
    lrw_pad = lrw_pad.at[0, SM_LRF:SM_LRF + GLA_LOWRANK].set(gla_lr_w[layer, 0])
    lrw_pad = lrw_pad.at[1, SM_LRB:SM_LRB + GLA_LOWRANK].set(gla_lr_w[layer, 1])
    lrb = gla_lr_b[layer][:, None, :]
    r3 = lambda t, l: t.reshape(b, l, t.shape[-1])
    s0_gla = jnp.zeros((b, 2, GLA_HEADS, GLA_DK, GLA_DV), F32)
    _, _, s_gla = _gla_call(r3(gqk_c, ctx_len), r3(gv_c, ctx_len), r3(sm_c, ctx_len), lrw_pad, lrb, s0_gla, ctx_len)
    gla_f, gla_b, _ = _gla_call(r3(gqk_l, seq), r3(gv_l, seq), r3(sm_l, seq), lrw_pad, lrb, s_gla, 256)

    lane_params = lambda p: jnp.zeros((1, LANES), F32).at[0, SM_A:SM_A + 2 * GDN_HEADS].set(p[layer].reshape(-1))
    alog_row, dtb_row = lane_params(gdn_a_log), lane_params(gdn_dt_bias)
    e64, e128 = _gdn_expand()
    s0_gdn = jnp.zeros(_gdn_state_shape(b), F32)
    one_col = lambda t: t.reshape(b, 1, ctx_len, t.shape[-1])
    _, _, s_gdn = _gdn_call(one_col(dqkv_c), one_col(sm_c), alog_row, dtb_row, e64, e128, s0_gdn, conv_w=conv_w)
    merge = lambda t: t.reshape(b, GRID_W // GDN_STEP_COLS, GDN_STEP_COLS * rows, t.shape[-1])
    gdn_f, gdn_b, _ = _gdn_call(merge(qkv_l), merge(smg_l), alog_row, dtb_row, e64, e128, s_gdn)
    gdn_f, gdn_b = (t.reshape(b, GRID_W, rows, d) for t in (gdn_f, gdn_b))

    flat = lambda t: t.reshape(b * seq, d)
    raster = lambda t: t.transpose(0, 2, 1, 3).reshape(b * seq, d)
    out = _tail_call(
        x.reshape(b * seq, d), flat(gla_f), flat(gla_b), gz_l, raster(gdn_f), raster(gdn_b), dz_l, og_l, mod_l,
        gla_norm_w[layer].reshape(1, d), gdn_norm_w[layer].reshape(1, d),
        norm2_w[layer][None, :], final_norm_w[None, :],
        w_out[layer].astype(BF16), ffn_w_gate[layer].astype(BF16), ffn_w_up[layer].astype(BF16),
        ffn_w_down[layer].astype(BF16), seq, tm)
    return out.reshape(b, seq, d)
```

```python
import functools

import numpy as np
import jax
import jax.numpy as jnp
from jax import lax
from jax.experimental import pallas as pl
from jax.experimental.pallas import tpu as pltpu

F32 = jnp.float32
BF16 = jnp.bfloat16
HI = lax.Precision.HIGHEST

EPS = 1e-6
CHUNK = 64
GRID_W = 64
CONV_W = 5
CONV_PAD = 8

GLA_HEADS = 4
GLA_DK = 128
GLA_DV = 256
GLA_LOWRANK = 16
GLA_GATE_NORM = 16.0
GDN_HEADS = 8
GDN_D = 128
GDN_GROUP = 4
WIDE = GDN_GROUP * CHUNK
GDN_STEP_COLS = 4

LANES = 128
VMEM_LIMIT = 56 * 1024 * 1024

SM_LRF, SM_LRB, SM_A, SM_B = 0, 16, 32, 48


def _dot(a, b, **kw):
    return jnp.dot(a, b, preferred_element_type=F32, **kw)


def _dot_nt(a, b):
    return lax.dot_general(a, b, (((1,), (1,)), ((), ())), preferred_element_type=F32)


def _dot_tn(a, b):
    return lax.dot_general(a, b, (((0,), (0,)), ((), ())), preferred_element_type=F32)


def _sigmoid(t):
    return 0.5 * jnp.tanh(0.5 * t) + 0.5


def _silu_of_half(h):
    return h * jnp.tanh(h) + h


def _silu(t):
    return _silu_of_half(0.5 * t)


def _softplus(t):
    return jnp.maximum(t, 0.0) + jnp.log(1.0 + jnp.exp(-jnp.abs(t)))


def _rms(t, w):
    return t * lax.rsqrt(jnp.mean(t * t, axis=-1, keepdims=True) + EPS) * w


def _params(n_grid):
    return pltpu.CompilerParams(dimension_semantics=("arbitrary",) * n_grid,
                                vmem_limit_bytes=VMEM_LIMIT)


def _resident(shape):
    return pl.BlockSpec(shape, lambda *_: (0,) * len(shape), pipeline_mode=pl.Buffered(1))


def _mod_kernel(c_ref, w_ref, b_ref, o_ref):
    o_ref[...] = _dot(_silu(c_ref[...]), w_ref[...], precision=HI) + b_ref[...]


def _mod_call(cc, w_mod, b_mod):
    rows, d = cc.shape
    n = w_mod.shape[1]
    return pl.pallas_call(
        _mod_kernel,
        out_shape=jax.ShapeDtypeStruct((rows, n), F32),
        grid=(n // d,),
        in_specs=[pl.BlockSpec((rows, d), lambda j: (0, 0)),
                  pl.BlockSpec((d, d), lambda j: (0, j)),
                  pl.BlockSpec((1, d), lambda j: (0, j))],
        out_specs=pl.BlockSpec((rows, d), lambda j: (0, j)),
        compiler_params=_params(1),
        name="mod",
    )(cc, w_mod, b_mod)


def _inproj_kernel(x_ref, shift_ref, scale_ref, nw_ref, w_ref, *out_refs, acts):
    h = _rms(x_ref[...], nw_ref[...]) * (1.0 + scale_ref[...]) + shift_ref[...]
    hb = h.astype(BF16)
    lo = 0
    for ref, act in zip(out_refs, acts):
        width = ref.shape[-1]
        y = _dot(hb, w_ref[:, lo:lo + width])
        ref[...] = (y if act is None else act(y)).astype(ref.dtype)
        lo += width


def _inproj_call(x2d, mod3, norm_w, weights, acts, tokens_per_mod, tm):
    n, d = x2d.shape
    widths = [w.shape[1] for w in weights]
    w_all = jnp.concatenate(weights, axis=1)
    mod_row = lambda i: (i * tm) // tokens_per_mod
    return pl.pallas_call(
        functools.partial(_inproj_kernel, acts=tuple(acts)),
        out_shape=[jax.ShapeDtypeStruct((n, w), F32 if w == LANES else BF16) for w in widths],
        grid=(n // tm,),
        in_specs=[pl.BlockSpec((tm, d), lambda i: (i, 0)),
                  pl.BlockSpec((None, 1, d), lambda i: (mod_row(i), 0, 0)),
                  pl.BlockSpec((None, 1, d), lambda i: (mod_row(i), 0, 1)),
                  _resident((1, d)),
                  _resident(w_all.shape)],
        out_specs=[pl.BlockSpec((tm, w), lambda i: (i, 0)) for w in widths],
        compiler_params=_params(1),
        name="inproj",
    )(x2d, mod3, mod3, norm_w, w_all)


GPROJ_COLS = 8
GPROJ_WIDTH = 2 * LANES


def _gproj_kernel(x_ref, shift_ref, scale_ref, nw_ref, w_ref, convw_ref, qkv_ref, smg_ref,
                  pad_ref, y_ref, *, rows):
    nq = GDN_HEADS * GDN_D
    n = rows * GPROJ_COLS
    halo = (CONV_W // 2) * GPROJ_COLS
    x = x_ref[...].reshape(n, x_ref.shape[-1])
    hb = (_rms(x, nw_ref[...]) * (1.0 + scale_ref[...]) + shift_ref[...]).astype(BF16)
    zeros = jnp.zeros((halo, GPROJ_WIDTH), F32)
    pad_ref[0:halo, :] = zeros
    pad_ref[halo + n:halo + n + halo, :] = zeros
    column = lambda slab, c: y_ref[slab, pl.ds(c, rows, stride=GPROJ_COLS), :]
    for lo in range(0, 3 * nq, GPROJ_WIDTH):
        pad_ref[halo:halo + n, :] = _dot(hb, w_ref[:, lo:lo + GPROJ_WIDTH])
        acc = None
        for i in range(CONV_W):
            start = halo + (i - CONV_W // 2) * GPROJ_COLS
            term = (0.5 * convw_ref[i:i + 1, lo:lo + GPROJ_WIDTH]) * pad_ref[start:start + n, :]
            acc = term if acc is None else acc + term
        y = _silu_of_half(acc)
        for slab in range(GPROJ_WIDTH // LANES):
            yh = y[:, slab * LANES:(slab + 1) * LANES]
            if lo < 2 * nq:
                scale = GDN_D ** -0.5 if lo < nq else 1.0
                yh = yh * (lax.rsqrt(jnp.sum(yh * yh, axis=-1, keepdims=True) + EPS) * scale)
            y_ref[slab] = yh
        for c in range(GPROJ_COLS):
            for slab in range(GPROJ_WIDTH // LANES):
                qkv_ref[c, :, lo + slab * LANES:lo + (slab + 1) * LANES] = column(slab, c).astype(BF16)
    y_ref[0] = _dot(hb, w_ref[:, 3 * nq:3 * nq + LANES])
    for c in range(GPROJ_COLS):
        smg_ref[c] = column(0, c)


def _gproj_call(x4, mod3, norm_w, w_g, conv_w):
    b, rows, ncol, d = x4.shape
    width = 3 * GDN_HEADS * GDN_D
    assert GDN_D == LANES and ncol % GPROJ_COLS == 0
    n = rows * GPROJ_COLS
    halo = (CONV_W // 2) * GPROJ_COLS
    return pl.pallas_call(
        functools.partial(_gproj_kernel, rows=rows),
        out_shape=[jax.ShapeDtypeStruct((b, ncol, rows, width), BF16),
                   jax.ShapeDtypeStruct((b, ncol, rows, LANES), F32)],
        grid=(b, ncol // GPROJ_COLS),
        in_specs=[pl.BlockSpec((None, rows, GPROJ_COLS, d), lambda bi, j: (bi, 0, j, 0)),
                  pl.BlockSpec((None, 1, d), lambda bi, j: (bi, 0, 0)),
                  pl.BlockSpec((None, 1, d), lambda bi, j: (bi, 0, 1)),
                  _resident((1, d)),
                  _resident(w_g.shape),
                  _resident(conv_w.shape)],
        out_specs=[pl.BlockSpec((None, GPROJ_COLS, rows, width), lambda bi, j: (bi, j, 0, 0)),
                   pl.BlockSpec((None, GPROJ_COLS, rows, LANES), lambda bi, j: (bi, j, 0, 0))],
        scratch_shapes=[pltpu.VMEM((n + 2 * halo, GPROJ_WIDTH), F32),
                        pltpu.VMEM((GPROJ_WIDTH // LANES, n, LANES), F32)],
        compiler_params=_params(2),
        name="gproj",
    )(x4, mod3, mod3, norm_w, w_g, conv_w)


def _tri(rev):
    row = lax.broadcasted_iota(jnp.int32, (CHUNK, CHUNK), 0)
    col = lax.broadcasted_iota(jnp.int32, (CHUNK, CHUNK), 1)
    return (col >= row) if rev else (col <= row)


def _split3(t):
    hi = t.astype(BF16)
    r1 = t - hi.astype(F32)
    mid = r1.astype(BF16)
    lo = (r1 - mid.astype(F32)).astype(BF16)
    return hi, mid, lo


def _dot3(a, b):
    a1 = a.astype(BF16)
    a2 = (a - a1.astype(F32)).astype(BF16)
    b1 = b.astype(BF16)
    b2 = (b - b1.astype(F32)).astype(BF16)
    return _dot(a1, b1) + _dot(a1, b2) + _dot(a2, b1)


def _gla_gates(d, sm_ref, lrw_ref, lrb_ref, g_ref):
    tb = sm_ref.shape[0]
    gpre = _dot3(sm_ref[...], lrw_ref[d]) + lrb_ref[d]
    g = (jnp.minimum(gpre, 0.0) - jnp.log(1.0 + jnp.exp(-jnp.abs(gpre)))) * (1.0 / GLA_GATE_NORM)
    row = lax.broadcasted_iota(jnp.int32, (tb, tb), 0)
    col = lax.broadcasted_iota(jnp.int32, (tb, tb), 1)
    order = (col >= row) if d == 1 else (col <= row)
    same_chunk = (row >> 6) == (col >> 6)
    tri = jnp.where(same_chunk, jnp.where(order, 1.0, 0.0), 0.0).astype(BF16)
    hi, mid, lo = _split3(g)
    g_ref[d] = _dot(tri, hi) + _dot(tri, mid) + _dot(tri, lo)


def _gla_local(d, c, qk_ref, v_ref, g_ref):
    rev = d == 1
    incl = _tri(rev)
    rows = slice(c * CHUNK, (c + 1) * CHUNK)
    qk = qk_ref[rows, :].astype(F32)
    v = v_ref[rows, :]
    G = g_ref[d, rows, :]
    g_mid = G[CHUNK // 2:CHUNK // 2 + 1, :]
    g_end = G[0:1, :] if rev else G[CHUNK - 1:CHUNK, :]
    e_g = jnp.exp(G)
    e_q = jnp.exp(G - g_mid)
    e_k = jnp.exp(g_mid - G)
    e_kend = jnp.exp(g_end - G)
    e_end = jnp.exp(g_end)
    nqk = GLA_HEADS * GLA_DK
    st = []
    for h in range(GLA_HEADS):
        sl = slice(h * GLA_DK, (h + 1) * GLA_DK)
        q = qk[:, sl] * (GLA_DK ** -0.5)
        k = qk[:, nqk + h * GLA_DK:nqk + (h + 1) * GLA_DK]
        st.append(dict(d=d, h=h, rows=rows, incl=incl,
                       v=v[:, h * GLA_DV:(h + 1) * GLA_DV],
                       qt=(q * e_q[:, sl]).astype(BF16), kt=(k * e_k[:, sl]).astype(BF16),
                       qg=(q * e_g[:, sl]).astype(BF16), ke=(k * e_kend[:, sl]).astype(BF16),
                       e_end=e_end[:, sl]))
    for s in st:
        s["a"] = jnp.where(s["incl"], _dot_nt(s["qt"], s["kt"]), 0.0).astype(BF16)
    for s in st:
        s["kv"] = _dot_tn(s["ke"], s["v"])
    return st


def _gla_recur(st, o_refs, s_ref):
    for s in st:
        s["s"] = s_ref[s["d"], s["h"]]
        s["o"] = _dot(s["a"], s["v"]) + _dot(s["qg"], s["s"].astype(BF16))
    for s in st:
        h = s["h"]
        o_refs[s["d"]][s["rows"], h * GLA_DV:(h + 1) * GLA_DV] = s["o"].astype(BF16)
        decay = jnp.broadcast_to(s["e_end"], (GLA_DK, GLA_DK)).T
        decay = jnp.concatenate([decay] * (GLA_DV // GLA_DK), axis=1)
        s_ref[s["d"], h] = decay * s["s"] + s["kv"]


def _gla_kernel(qkf, vf, smf, qkb, vb, smb, lrw, lrb, s0, of, ob, sfin, s_ref, g_ref, *, nchunk):
    i = pl.program_id(1)

    @pl.when(i == 0)
    def _():
        s_ref[...] = s0[...]

    _gla_gates(0, smf, lrw, lrb, g_ref)
    _gla_gates(1, smb, lrw, lrb, g_ref)
    local = lambda t: (_gla_local(0, t, qkf, vf, g_ref) + _gla_local(1, nchunk - 1 - t, qkb, vb, g_ref))
    pending = local(0)
    for t in range(nchunk):
        ahead = local(t + 1) if t + 1 < nchunk else None
        _gla_recur(pending, (of, ob), s_ref)
        pending = ahead

    @pl.when(i == pl.num_programs(1) - 1)
    def _():
        sfin[...] = s_ref[...]


def _gla_call(gqk, gv, sm, lrw_pad, lrb, s0, tb):
    b, l, _ = gqk.shape
    nblk = l // tb
    fwd = lambda bi, i: (bi, i, 0)
    bwd = lambda bi, i: (bi, nblk - 1 - i, 0)
    sspec = pl.BlockSpec((None, 2, GLA_HEADS, GLA_DK, GLA_DV), lambda bi, i: (bi, 0, 0, 0, 0))
    wide = GLA_HEADS * GLA_DV
    return pl.pallas_call(
        functools.partial(_gla_kernel, nchunk=tb // CHUNK),
        out_shape=[jax.ShapeDtypeStruct((b, l, wide), BF16),
                   jax.ShapeDtypeStruct((b, l, wide), BF16),
                   jax.ShapeDtypeStruct(s0.shape, F32)],
        grid=(b, nblk),
        in_specs=[pl.BlockSpec((None, tb, 2 * GLA_HEADS * GLA_DK), fwd),
                  pl.BlockSpec((None, tb, wide), fwd),
                  pl.BlockSpec((None, tb, LANES), fwd),
                  pl.BlockSpec((None, tb, 2 * GLA_HEADS * GLA_DK), bwd),
                  pl.BlockSpec((None, tb, wide), bwd),
                  pl.BlockSpec((None, tb, LANES), bwd),
                  _resident(lrw_pad.shape),
                  _resident(lrb.shape),
                  sspec],
        out_specs=[pl.BlockSpec((None, tb, wide), fwd),
                   pl.BlockSpec((None, tb, wide), bwd),
                   sspec],
        scratch_shapes=[pltpu.VMEM((2, GLA_HEADS, GLA_DK, GLA_DV), F32),
                        pltpu.VMEM((2, tb, GLA_HEADS * GLA_DK), F32)],
        compiler_params=_params(2),
        name="gla",
    )(gqk, gv, sm, gqk, gv, sm, lrw_pad, lrb, s0)


def _gdn_conv(x_ref, convw_ref, xp_ref, qkv_ref):
    rows = x_ref.shape[0]
    nq = GDN_HEADS * GDN_D
    width = 3 * nq
    zeros = jnp.zeros((CONV_PAD, width), F32)
    xp_ref[0:CONV_PAD, :] = zeros
    xp_ref[CONV_PAD + rows:CONV_PAD + rows + CONV_PAD, :] = zeros
    xp_ref[CONV_PAD:CONV_PAD + rows, :] = x_ref[...].astype(F32)
    acc = None
    for i in range(CONV_W):
        start = CONV_PAD + i - CONV_W // 2
        term = convw_ref[i:i + 1, :] * xp_ref[start:start + rows, :]
        acc = term if acc is None else acc + term
    xs = _silu(acc)
    for h in range(GDN_HEADS):
        qh = xs[:, h * GDN_D:(h + 1) * GDN_D]
        kh = xs[:, nq + h * GDN_D:nq + (h + 1) * GDN_D]
        qn = qh * (lax.rsqrt(jnp.sum(qh * qh, axis=-1, keepdims=True) + EPS) * GDN_D ** -0.5)
        kn = kh * lax.rsqrt(jnp.sum(kh * kh, axis=-1, keepdims=True) + EPS)
        qkv_ref[:, h * GDN_D:(h + 1) * GDN_D] = qn.astype(BF16)
        qkv_ref[:, nq + h * GDN_D:nq + (h + 1) * GDN_D] = kn.astype(BF16)
    qkv_ref[:, 2 * nq:] = xs[:, 2 * nq:].astype(BF16)


def _dot_exact(t, onehot):
    hi, mid, lo = _split3(t)
    return _dot(hi, onehot) + _dot(mid, onehot) + _dot(lo, onehot)


def _chunk_end(rev, c):
    return c * CHUNK if rev else (c + 1) * CHUNK - 1


def _gdn_gates(d, sm_ref, alog_ref, dtb_ref, e64_ref, e128_ref, gi_ref, gt_ref, bt_ref, ge_ref):
    rev = d == 1
    rows = sm_ref.shape[0]
    sm = sm_ref[...]
    gate = -jnp.exp(alog_ref[...]) * _softplus(sm + dtb_ref[...])
    tri = jnp.where(_tri(rev), 1.0, 0.0).astype(BF16)
    nchunk = rows // CHUNK
    gcum = jnp.concatenate([_dot_exact_lhs(tri, gate[c * CHUNK:(c + 1) * CHUNK, :]) for c in range(nchunk)],
                           axis=0)
    gi_ref[d] = _dot_exact(gcum, e64_ref[d])
    gt_ref[d] = gcum.T
    bt_ref[d] = _sigmoid(sm).T
    ends = [gcum[_chunk_end(rev, c):_chunk_end(rev, c) + 1, :] for c in range(nchunk)]
    if nchunk < ge_ref.shape[1]:
        ends.append(jnp.zeros((ge_ref.shape[1] - nchunk, LANES), F32))
    ge_ref[d] = _dot_exact(jnp.concatenate(ends, axis=0), e128_ref[d])


def _dot_exact_lhs(onehot, t):
    hi, mid, lo = _split3(t)
    return _dot(onehot, hi) + _dot(onehot, mid) + _dot(onehot, lo)


def _bd_wide(m):
    low = lax.broadcasted_iota(jnp.int32, (CHUNK, LANES), 1) < CHUNK
    zero = jnp.zeros((CHUNK, LANES), m.dtype)
    rows = []
    for a in range(GDN_GROUP):
        tile = m[:, (a // 2) * LANES:(a // 2 + 1) * LANES]
        kept = jnp.where(low, tile, zero) if a % 2 == 0 else jnp.where(low, zero, tile)
        rows.append(jnp.concatenate([kept, zero] if a < 2 else [zero, kept], axis=1))
    return jnp.concatenate(rows, axis=0)


def _bd_heads(t):
    zero = jnp.zeros((CHUNK, GDN_D), t.dtype)
    return jnp.concatenate(
        [jnp.concatenate([t[:, a * GDN_D:(a + 1) * GDN_D] if b == a else zero for b in range(GDN_GROUP)], axis=1)
         for a in range(GDN_GROUP)], axis=0)


def _gdn_local(groups, qkv_refs, gi_ref, gt_ref, bt_ref, ge_ref):
    nq = GDN_HEADS * GDN_D
    span = GDN_GROUP * GDN_D
    row = lax.broadcasted_iota(jnp.int32, (CHUNK, WIDE), 0)
    col = lax.broadcasted_iota(jnp.int32, (CHUNK, WIDE), 1) & (CHUNK - 1)
    eye = row == col
    eye_f = jnp.where(eye, 1.0, 0.0)

    def row_of(t_ref, base, d, hg, rs):
        lane = lambda a: base + GDN_HEADS * d + GDN_GROUP * hg + a
        return jnp.concatenate([t_ref[d, lane(a):lane(a) + 1, rs] for a in range(GDN_GROUP)], axis=1)

    st = []
    for d, c, hg in groups:
        rev = d == 1
        rs = slice(c * CHUNK, (c + 1) * CHUNK)
        end = _chunk_end(rev, c)
        qkv_ref = qkv_refs[d]
        wl = slice(hg * WIDE, (hg + 1) * WIDE)
        st.append(dict(
            d=d, hg=hg, rs=rs,
            strict=(col > row) if rev else (col < row),
            q=qkv_ref[rs, hg * span:(hg + 1) * span],
            k=qkv_ref[rs, nq + hg * span:nq + (hg + 1) * span],
            v=qkv_ref[rs, 2 * nq + hg * span:2 * nq + (hg + 1) * span],
            gi=gi_ref[d, rs, wl],
            gj=row_of(gt_ref, SM_A, d, hg, rs),
            bj=row_of(bt_ref, SM_B, d, hg, rs),
            g_end=gi_ref[d, end:end + 1, wl],
            s_decay=ge_ref[d, c:c + 1, hg * span:(hg + 1) * span]))
    for s in st:
        s["bdk"] = _bd_heads(s["k"])
        s["qkk"] = _dot_nt(jnp.concatenate([s["q"], s["k"]], axis=0), s["bdk"])
    for s in st:
        dec = jnp.exp(jnp.where(s["strict"], s["gi"] - s["gj"], -1e30))
        s["dec"] = dec
        x = -(s["qkk"][CHUNK:] * dec * s["bj"])
        s["x"] = x
        s["p"] = eye_f + jnp.where((row >> 1) == (col >> 1), x, 0.0)
    for level in range(1, 6):
        sibling = ((row >> level) ^ 1) == (col >> level)
        for s in st:
            s["pb"] = s["p"].astype(BF16)
            s["t"] = _dot(jnp.where(sibling, s["x"], 0.0).astype(BF16), _bd_wide(s["pb"]))
        for s in st:
            s["p"] = s["p"] + _dot(s["pb"], _bd_wide(s["t"].astype(BF16)))
    for s in st:
        e_gj = jnp.exp(s["gj"])
        s["rv"] = _dot(s["p"].astype(BF16), _bd_heads(s["v"]))
        s["w"] = _dot((s["p"] * e_gj).astype(BF16), s["bdk"]).astype(BF16)
        s["e"] = jnp.where(eye, e_gj, 0.0).astype(BF16)
        a_beta = s["qkk"][:CHUNK] * (s["dec"] + eye_f) * s["bj"]
        kscale = jnp.where(eye, jnp.exp(s["g_end"] - s["gj"]) * s["bj"], 0.0)
        s["ak"] = jnp.concatenate([a_beta, kscale], axis=0).astype(BF16)
    return st


def _gdn_recur(st, o_refs, s_ref):
    pair = 2 * GDN_D
    zero = jnp.zeros((GDN_D, GDN_D), BF16)
    for s in st:
        s["s"] = s_ref[s["d"], s["hg"]]
        sb = s["s"].astype(BF16)
        out = []
        for p in range(GDN_GROUP // 2):
            lanes = slice(p * pair, (p + 1) * pair)
            s_a, s_b = sb[:, p * pair:p * pair + GDN_D], sb[:, p * pair + GDN_D:(p + 1) * pair]
            bd = jnp.concatenate([jnp.concatenate([s_a, zero], axis=1),
                                  jnp.concatenate([zero, s_b], axis=1)], axis=0)
            out.append(_dot(jnp.concatenate([s["w"][:, lanes], s["q"][:, lanes]], axis=0), bd))
        s["wqs"] = jnp.concatenate(out, axis=1)
    for s in st:
        vhat = (s["rv"] - s["wqs"][:CHUNK]).astype(BF16)
        s["av"] = _dot(s["ak"], _bd_heads(vhat))
        s["eq"] = _dot(s["e"], _bd_heads(s["wqs"][CHUNK:].astype(BF16)))
    for s in st:
        span = GDN_GROUP * GDN_D
        o_refs[s["d"]][s["rs"], s["hg"] * span:(s["hg"] + 1) * span] = (s["eq"] + s["av"][:CHUNK]).astype(BF16)
        k_rows = jnp.concatenate([s["k"][:, a * GDN_D:(a + 1) * GDN_D] for a in range(GDN_GROUP)], axis=0)
        update = _dot_tn(k_rows, _bd_heads(s["av"][CHUNK:].astype(BF16)))
        s_ref[s["d"], s["hg"]] = jnp.exp(s["s_decay"]) * s["s"] + update


def _gdn_scan(qkv_refs, smf, smb, alog, dtb, e64, e128, s0, of, ob, sfin, s_ref, gi_ref, gt_ref, bt_ref, ge_ref):
    i = pl.program_id(1)

    @pl.when(i == 0)
    def _():
        s_ref[...] = s0[...]

    _gdn_gates(0, smf, alog, dtb, e64, e128, gi_ref, gt_ref, bt_ref, ge_ref)
    _gdn_gates(1, smb, alog, dtb, e64, e128, gi_ref, gt_ref, bt_ref, ge_ref)
    nchunk = smf.shape[0] // CHUNK
    ngroup = GDN_HEADS // GDN_GROUP
    step_groups = lambda t: ([(0, t, hg) for hg in range(ngroup)]
                             + [(1, nchunk - 1 - t, hg) for hg in range(ngroup)])
    local = _gdn_local([g for t in range(nchunk) for g in step_groups(t)], qkv_refs, gi_ref, gt_ref, bt_ref, ge_ref)
    per_step = 2 * ngroup
    for t in range(nchunk):
        _gdn_recur(local[t * per_step:(t + 1) * per_step], (of, ob), s_ref)

    @pl.when(i == pl.num_programs(1) - 1)
    def _():
        sfin[...] = s_ref[...]


def _gdn_kernel(qkvf, smf, qkvb, smb, alog, dtb, e64, e128, s0, of, ob, sfin, s_ref, gi_ref, gt_ref, bt_ref, ge_ref):
    _gdn_scan((qkvf, qkvb), smf, smb, alog, dtb, e64, e128, s0, of, ob, sfin, s_ref, gi_ref, gt_ref, bt_ref, ge_ref)


def _gdn_raw_kernel(xf, smf, xb, smb, convw, alog, dtb, e64, e128, s0, of, ob, sfin,
                    s_ref, gi_ref, gt_ref, bt_ref, ge_ref, xp_ref, qkv_ref):
    _gdn_conv(xf, convw, xp_ref, qkv_ref.at[0])
    _gdn_conv(xb, convw, xp_ref, qkv_ref.at[1])
    _gdn_scan((qkv_ref.at[0], qkv_ref.at[1]), smf, smb, alog, dtb, e64, e128, s0, of, ob, sfin,
              s_ref, gi_ref, gt_ref, bt_ref, ge_ref)


def _gdn_state_shape(b):
    return (b, 2, GDN_HEADS // GDN_GROUP, GDN_D, GDN_GROUP * GDN_D)


def _gdn_call(qkv, sm, alog_row, dtb_row, e64, e128, s0, conv_w=None):
    b, ncol, rows, width = qkv.shape
    wide = GDN_HEADS * GDN_D
    nchunk = rows // CHUNK
    fwd = lambda bi, i: (bi, i, 0, 0)
    bwd = lambda bi, i: (bi, ncol - 1 - i, 0, 0)
    slab = lambda w, index: pl.BlockSpec((None, None, rows, w), index)
    sspec = pl.BlockSpec((None,) + s0.shape[1:], lambda bi, i: (bi, 0, 0, 0, 0))
    params = [alog_row, dtb_row, e64, e128]
    scratch = [pltpu.VMEM(s0.shape[1:], F32),
               pltpu.VMEM((2, rows, GDN_HEADS * CHUNK), F32),
               pltpu.VMEM((2, LANES, rows), F32),
               pltpu.VMEM((2, LANES, rows), F32),
               pltpu.VMEM((2, 8, wide), F32)]
    assert nchunk <= 8
    body = _gdn_kernel
    if conv_w is not None:
        params = [conv_w] + params
        scratch += [pltpu.VMEM((rows + 2 * CONV_PAD, width), F32),
                    pltpu.VMEM((2, rows, width), BF16)]
        body = _gdn_raw_kernel
    return pl.pallas_call(
        body,
        out_shape=[jax.ShapeDtypeStruct((b, ncol, rows, wide), BF16),
                   jax.ShapeDtypeStruct((b, ncol, rows, wide), BF16),
                   jax.ShapeDtypeStruct(s0.shape, F32)],
        grid=(b, ncol),
        in_specs=[slab(width, fwd), slab(LANES, fwd), slab(width, bwd), slab(LANES, bwd)]
                 + [_resident(p.shape) for p in params] + [sspec],
        out_specs=[slab(wide, fwd), slab(wide, bwd), sspec],
        scratch_shapes=scratch,
        compiler_params=_params(2),
        name="gdn",
    )(qkv, sm, qkv, sm, *params, s0)


def _head_norm(o, heads, gain):
    dv = o.shape[-1] // heads
    parts = []
    for h in range(heads):
        oh = o[:, h * dv:(h + 1) * dv]
        parts.append(oh * lax.rsqrt(jnp.mean(oh * oh, axis=-1, keepdims=True) + EPS))
    return jnp.concatenate(parts, axis=-1) * gain


def _tail_kernel(x_ref, glaf, glab, gz, gdnf, gdnb, dz, og, gate1, shift2, scale2, gate2,
                 gla_gain, gdn_gain, n2w, fw, w_out, w_gate, w_up, w_down, out_ref):
    d = x_ref.shape[-1]
    f32 = lambda ref: ref[...].astype(F32)
    y_gla = _head_norm(f32(glaf) + f32(glab), GLA_HEADS, gla_gain[...]) * f32(gz)
    y_gdn = _head_norm(f32(gdnf) + f32(gdnb), GDN_HEADS, gdn_gain[...]) * f32(dz)
    gates = f32(og)
    merged = gates[:, :d] * y_gla + gates[:, d:] * y_gdn
    x1 = x_ref[...] + gate1[...] * _dot(merged.astype(BF16), w_out[...])
    h2 = (_rms(x1, n2w[...]) * (1.0 + scale2[...]) + shift2[...]).astype(BF16)
    hidden = _silu(_dot(h2, w_gate[...])) * _dot(h2, w_up[...])
    x2 = x1 + gate2[...] * _dot(hidden.astype(BF16), w_down[...])
    out_ref[...] = _rms(x2, fw[...])


def _tail_call(x2d, glaf, glab, gz, gdnf, gdnb, dz, og, mod3, gla_gain, gdn_gain, n2w, fw,
               w_out, w_gate, w_up, w_down, tokens_per_mod, tm):
    n, d = x2d.shape
    mod_row = lambda i: (i * tm) // tokens_per_mod
    tok = lambda w: pl.BlockSpec((tm, w), lambda i: (i, 0))
    modspec = lambda j: pl.BlockSpec((None, 1, d), lambda i: (mod_row(i), 0, j))
    return pl.pallas_call(
        _tail_kernel,
        out_shape=jax.ShapeDtypeStruct((n, d), F32),
        grid=(n // tm,),
        in_specs=[tok(d)] * 7 + [tok(2 * d)]
                 + [modspec(2), modspec(3), modspec(4), modspec(5)]
                 + [_resident(a.shape) for a in (gla_gain, gdn_gain, n2w, fw, w_out, w_gate, w_up, w_down)],
        out_specs=tok(d),
        compiler_params=_params(1),
        name="tail",
    )(x2d, glaf, glab, gz, gdnf, gdnb, dz, og, mod3, mod3, mod3, mod3,
      gla_gain, gdn_gain, n2w, fw, w_out, w_gate, w_up, w_down)


def _split_w_in(w_in):
    d = w_in.shape[0]
    gq, gk, gv, gz, lrf, lrb, dq, dk, dv, dz, ab, p11, p12 = jnp.split(
        w_in, np.cumsum([512, 512, 1024, 1024, 16, 16, 1024, 1024, 1024, 1024, 32, 1024]).tolist(), axis=1)
    pad = jnp.zeros((d, LANES - 2 * GLA_LOWRANK - 4 * GDN_HEADS), w_in.dtype)
    cat = lambda *parts: jnp.concatenate(parts, axis=1).astype(BF16)
    return dict(gqk=cat(gq, gk), gv=cat(gv), gz=cat(gz), dqkv=cat(dq, dk, dv), dz=cat(dz),
                og=cat(p11, p12), sm=cat(lrf, lrb, ab, pad))


def _gdn_expand():
    e64 = np.zeros((2, LANES, GDN_HEADS * CHUNK), np.float32)
    e128 = np.zeros((2, LANES, GDN_HEADS * GDN_D), np.float32)
    for d in range(2):
        for h in range(GDN_HEADS):
            e64[d, SM_A + GDN_HEADS * d + h, h * CHUNK:(h + 1) * CHUNK] = 1.0
            e128[d, SM_A + GDN_HEADS * d + h, h * GDN_D:(h + 1) * GDN_D] = 1.0
    return jnp.asarray(e64, dtype=BF16), jnp.asarray(e128, dtype=BF16)


def kernel(x, c, ctx, c_ctx, w_mod, b_mod, norm1_w, norm2_w, w_in, gla_lr_w, gla_lr_b, gla_norm_w, gdn_conv_w, gdn_a_log, gdn_dt_bias, gdn_norm_w, w_out, ffn_w_gate, ffn_w_up, ffn_w_down, final_norm_w):
    b, seq, d = x.shape
    ctx_len = ctx.shape[1]
    depth = w_mod.shape[0]
    assert depth == 1, "only the single-layer block is implemented"
    assert d == GLA_HEADS * GLA_DV == GDN_HEADS * GDN_D
    rows = seq // GRID_W
    assert seq % GRID_W == 0 and rows % CHUNK == 0 and ctx_len % CHUNK == 0
    layer = 0

    mod_rows = -(-(b + 1) // 8) * 8
    cc = jnp.zeros((mod_rows, d), F32).at[:b].set(c).at[b].set(c_ctx)
    mod = _mod_call(cc, w_mod[layer], b_mod[layer][None, :])
    mod_l = mod[:b, None, :]
    mod_c = mod[b:b + 1, None, :]

    w = _split_w_in(w_in[layer])
    n1w = norm1_w[layer][None, :]
    conv_w = gdn_conv_w[layer]
    tm = 256
    lat = (("gqk", None), ("gv", None), ("gz", _silu), ("dz", _silu), ("og", _sigmoid), ("sm", None))
    gqk_l, gv_l, gz_l, dz_l, og_l, sm_l = _inproj_call(
        x.reshape(b * seq, d), mod_l, n1w, [w[k] for k, _ in lat], [a for _, a in lat], seq, tm)
    gqk_c, gv_c, dqkv_c, sm_c = _inproj_call(
        ctx.reshape(b * ctx_len, d), mod_c, n1w, [w[k] for k in ("gqk", "gv", "dqkv", "sm")], [None] * 4,
        b * ctx_len, tm)
    qkv_l, smg_l = _gproj_call(x.reshape(b, rows, GRID_W, d), mod_l, n1w,
                               jnp.concatenate([w["dqkv"], w["sm"]], axis=1), conv_w)

    lrw_pad = jnp.zeros((2, LANES, GLA_HEADS * GLA_DK), F32)
    lrw_pad = lrw_pad.at[0, SM_LRF:SM_LRF + GLA_LOWRANK].set(gla_lr_w[layer, 0])
    lrw_pad = lrw_pad.at[1, SM_LRB:SM_LRB + GLA_LOWRANK].set(gla_lr_w[layer, 1])
    lrb = gla_lr_b[layer][:, None, :]
    r3 = lambda t, l: t.reshape(b, l, t.shape[-1])
    s0_gla = jnp.zeros((b, 2, GLA_HEADS, GLA_DK, GLA_DV), F32)
    _, _, s_gla = _gla_call(r3(gqk_c, ctx_len), r3(gv_c, ctx_len), r3(sm_c, ctx_len), lrw_pad, lrb, s0_gla, ctx_len)
    gla_f, gla_b, _ = _gla_call(r3(gqk_l, seq), r3(gv_l, seq), r3(sm_l, seq), lrw_pad, lrb, s_gla, 256)

    lane_params = lambda p: jnp.zeros((1, LANES), F32).at[0, SM_A:SM_A + 2 * GDN_HEADS].set(p[layer].reshape(-1))
    alog_row, dtb_row = lane_params(gdn_a_log), lane_params(gdn_dt_bias)
    e64, e128 = _gdn_expand()
    s0_gdn = jnp.zeros(_gdn_state_shape(b), F32)
    one_col = lambda t: t.reshape(b, 1, ctx_len, t.shape[-1])
    _, _, s_gdn = _gdn_call(one_col(dqkv_c), one_col(sm_c), alog_row, dtb_row, e64, e128, s0_gdn, conv_w=conv_w)
    merge = lambda t: t.reshape(b, GRID_W // GDN_STEP_COLS, GDN_STEP_COLS * rows, t.shape[-1])
    gdn_f, gdn_b, _ = _gdn_call(merge(qkv_l), merge(smg_l), alog_row, dtb_row, e64, e128, s_gdn)
    gdn_f, gdn_b = (t.reshape(b, GRID_W, rows, d) for t in (gdn_f, gdn_b))

    flat = lambda t: t.reshape(b * seq, d)
    raster = lambda t: t.transpose(0, 2, 1, 3).reshape(b * seq, d)
    out = _tail_call(
        x.reshape(b * seq, d), flat(gla_f), flat(gla_b), gz_l, raster(gdn_f), raster(gdn_b), dz_l, og_l, mod_l,
        gla_norm_w[layer].reshape(1, d), gdn_norm_w[layer].reshape(1, d),
        norm2_w[layer][None, :], final_norm_w[None, :],
        w_out[layer].astype(BF16), ffn_w_gate[layer].astype(BF16), ffn_w_up[layer].astype(BF16),
        ffn_w_down[layer].astype(BF16), seq, tm)
    return out.reshape(b, seq, d)
```

```python
import functools

import numpy as np
import jax
import jax.numpy as jnp
from jax import lax
from jax.experimental import pallas as pl
from jax.experimental.pallas import tpu as pltpu

F32 = jnp.float32
BF16 = jnp.bfloat16
HI = lax.Precision.HIGHEST

EPS = 1e-6
CHUNK = 64
GRID_W = 64
CONV_W = 5
CONV_PAD = 8

GLA_HEADS = 4
GLA_DK = 128
GLA_DV = 256
GLA_LOWRANK = 16
GLA_GATE_NORM = 16.0
GDN_HEADS = 8
GDN_D = 128
GDN_GROUP = 4
WIDE = GDN_GROUP * CHUNK
GDN_STEP_COLS = 4

LANES = 128
VMEM_LIMIT = 56 * 1024 * 1024

SM_LRF, SM_LRB, SM_A, SM_B = 0, 16, 32, 48


def _dot(a, b, **kw):
    return jnp.dot(a, b, preferred_element_type=F32, **kw)


def _dot_nt(a, b):
    return lax.dot_general(a, b, (((1,), (1,)), ((), ())), preferred_element_type=F32)


def _dot_tn(a, b):
    return lax.dot_general(a, b, (((0,), (0,)), ((), ())), preferred_element_type=F32)


def _sigmoid(t):
    return 0.5 * jnp.tanh(0.5 * t) + 0.5


def _silu_of_half(h):
    return h * jnp.tanh(h) + h


def _silu(t):
    return _silu_of_half(0.5 * t)


def _softplus(t):
    return jnp.maximum(t, 0.0) + jnp.log(1.0 + jnp.exp(-jnp.abs(t)))


def _rms(t, w):
    return t * lax.rsqrt(jnp.mean(t * t, axis=-1, keepdims=True) + EPS) * w


def _params(n_grid):
    return pltpu.CompilerParams(dimension_semantics=("arbitrary",) * n_grid,
                                vmem_limit_bytes=VMEM_LIMIT)


def _resident(shape):
    return pl.BlockSpec(shape, lambda *_: (0,) * len(shape), pipeline_mode=pl.Buffered(1))


def _mod_kernel(c_ref, w_ref, b_ref, o_ref):
    o_ref[...] = _dot(_silu(c_ref[...]), w_ref[...], precision=HI) + b_ref[...]


def _mod_call(cc, w_mod, b_mod):
    rows, d = cc.shape
    n = w_mod.shape[1]
    return pl.pallas_call(
        _mod_kernel,
        out_shape=jax.ShapeDtypeStruct((rows, n), F32),
        grid=(n // d,),
        in_specs=[pl.BlockSpec((rows, d), lambda j: (0, 0)),
                  pl.BlockSpec((d, d), lambda j: (0, j)),
                  pl.BlockSpec((1, d), lambda j: (0, j))],
        out_specs=pl.BlockSpec((rows, d), lambda j: (0, j)),
        compiler_params=_params(1),
        name="mod",
    )(cc, w_mod, b_mod)


def _inproj_kernel(x_ref, shift_ref, scale_ref, nw_ref, w_ref, *out_refs, acts):
    h = _rms(x_ref[...], nw_ref[...]) * (1.0 + scale_ref[...]) + shift_ref[...]
    hb = h.astype(BF16)
    lo = 0
    for ref, act in zip(out_refs, acts):
        width = ref.shape[-1]
        y = _dot(hb, w_ref[:, lo:lo + width])
        ref[...] = (y if act is None else act(y)).astype(ref.dtype)
        lo += width


def _inproj_call(x2d, mod3, norm_w, weights, acts, tokens_per_mod, tm):
    n, d = x2d.shape
    widths = [w.shape[1] for w in weights]
    w_all = jnp.concatenate(weights, axis=1)
    mod_row = lambda i: (i * tm) // tokens_per_mod
    return pl.pallas_call(
        functools.partial(_inproj_kernel, acts=tuple(acts)),
        out_shape=[jax.ShapeDtypeStruct((n, w), F32 if w == LANES else BF16) for w in widths],
        grid=(n // tm,),
        in_specs=[pl.BlockSpec((tm, d), lambda i: (i, 0)),
                  pl.BlockSpec((None, 1, d), lambda i: (mod_row(i), 0, 0)),
                  pl.BlockSpec((None, 1, d), lambda i: (mod_row(i), 0, 1)),
                  _resident((1, d)),
                  _resident(w_all.shape)],
        out_specs=[pl.BlockSpec((tm, w), lambda i: (i, 0)) for w in widths],
        compiler_params=_params(1),
        name="inproj",
    )(x2d, mod3, mod3, norm_w, w_all)


GPROJ_COLS = 8
GPROJ_WIDTH = 2 * LANES


def _gproj_kernel(x_ref, shift_ref, scale_ref, nw_ref, w_ref, convw_ref, qkv_ref, smg_ref,
                  pad_ref, y_ref, *, rows):
    nq = GDN_HEADS * GDN_D
    n = rows * GPROJ_COLS
    halo = (CONV_W // 2) * GPROJ_COLS
    x = x_ref[...].reshape(n, x_ref.shape[-1])
    hb = (_rms(x, nw_ref[...]) * (1.0 + scale_ref[...]) + shift_ref[...]).astype(BF16)
    zeros = jnp.zeros((halo, GPROJ_WIDTH), F32)
    pad_ref[0:halo, :] = zeros
    pad_ref[halo + n:halo + n + halo, :] = zeros
    column = lambda slab, c: y_ref[slab, pl.ds(c, rows, stride=GPROJ_COLS), :]
    for lo in range(0, 3 * nq, GPROJ_WIDTH):
        pad_ref[halo:halo + n, :] = _dot(hb, w_ref[:, lo:lo + GPROJ_WIDTH])
        acc = None
        for i in range(CONV_W):
            start = halo + (i - CONV_W // 2) * GPROJ_COLS
            term = (0.5 * convw_ref[i:i + 1, lo:lo + GPROJ_WIDTH]) * pad_ref[start:start + n, :]
            acc = term if acc is None else acc + term
        y = _silu_of_half(acc)
        for slab in range(GPROJ_WIDTH // LANES):
            yh = y[:, slab * LANES:(slab + 1) * LANES]
            if lo < 2 * nq:
                scale = GDN_D ** -0.5 if lo < nq else 1.0
                yh = yh * (lax.rsqrt(jnp.sum(yh * yh, axis=-1, keepdims=True) + EPS) * scale)
            y_ref[slab] = yh
        for c in range(GPROJ_COLS):
            for slab in range(GPROJ_WIDTH // LANES):
                qkv_ref[c, :, lo + slab * LANES:lo + (slab + 1) * LANES] = column(slab, c).astype(BF16)
    y_ref[0] = _dot(hb, w_ref[:, 3 * nq:3 * nq + LANES])
    for c in range(GPROJ_COLS):
        smg_ref[c] = column(0, c)


def _gproj_call(x4, mod3, norm_w, w_g, conv_w):
    b, rows, ncol, d = x4.shape
    width = 3 * GDN_HEADS * GDN_D
    assert GDN_D == LANES and ncol % GPROJ_COLS == 0
    n = rows * GPROJ_COLS
    halo = (CONV_W // 2) * GPROJ_COLS
    return pl.pallas_call(
        functools.partial(_gproj_kernel, rows=rows),
        out_shape=[jax.ShapeDtypeStruct((b, ncol, rows, width), BF16),
                   jax.ShapeDtypeStruct((b, ncol, rows, LANES), F32)],
        grid=(b, ncol // GPROJ_COLS),
        in_specs=[pl.BlockSpec((None, rows, GPROJ_COLS, d), lambda bi, j: (bi, 0, j, 0)),
                  pl.BlockSpec((None, 1, d), lambda bi, j: (bi, 0, 0)),
                  pl.BlockSpec((None, 1, d), lambda bi, j: (bi, 0, 1)),
                  _resident((1, d)),
                  _resident(w_g.shape),
                  _resident(conv_w.shape)],
        out_specs=[pl.BlockSpec((None, GPROJ_COLS, rows, width), lambda bi, j: (bi, j, 0, 0)),
                   pl.BlockSpec((None, GPROJ_COLS, rows, LANES), lambda bi, j: (bi, j, 0, 0))],
        scratch_shapes=[pltpu.VMEM((n + 2 * halo, GPROJ_WIDTH), F32),
                        pltpu.VMEM((GPROJ_WIDTH // LANES, n, LANES), F32)],
        compiler_params=_params(2),
        name="gproj",
    )(x4, mod3, mod3, norm_w, w_g, conv_w)


def _tri(rev):
    row = lax.broadcasted_iota(jnp.int32, (CHUNK, CHUNK), 0)
    col = lax.broadcasted_iota(jnp.int32, (CHUNK, CHUNK), 1)
    return (col >= row) if rev else (col <= row)


def _split3(t):
    hi = t.astype(BF16)
    r1 = t - hi.astype(F32)
    mid = r1.astype(BF16)
    lo = (r1 - mid.astype(F32)).astype(BF16)
    return hi, mid, lo


def _dot3(a, b):
    a1 = a.astype(BF16)
    a2 = (a - a1.astype(F32)).astype(BF16)
    b1 = b.astype(BF16)
    b2 = (b - b1.astype(F32)).astype(BF16)
    return _dot(a1, b1) + _dot(a1, b2) + _dot(a2, b1)


def _gla_gates(d, sm_ref, lrw_ref, lrb_ref, g_ref):
    tb = sm_ref.shape[0]
    gpre = _dot3(sm_ref[...], lrw_ref[d]) + lrb_ref[d]
    g = (jnp.minimum(gpre, 0.0) - jnp.log(1.0 + jnp.exp(-jnp.abs(gpre)))) * (1.0 / GLA_GATE_NORM)
    row = lax.broadcasted_iota(jnp.int32, (tb, tb), 0)
    col = lax.broadcasted_iota(jnp.int32, (tb, tb), 1)
    order = (col >= row) if d == 1 else (col <= row)
    same_chunk = (row >> 6) == (col >> 6)
    tri = jnp.where(same_chunk, jnp.where(order, 1.0, 0.0), 0.0).astype(BF16)
    hi, mid, lo = _split3(g)
    g_ref[d] = _dot(tri, hi) + _dot(tri, mid) + _dot(tri, lo)


def _gla_local(d, c, qk_ref, v_ref, g_ref):
    rev = d == 1
    incl = _tri(rev)
    rows = slice(c * CHUNK, (c + 1) * CHUNK)
    qk = qk_ref[rows, :].astype(F32)
    v = v_ref[rows, :]
    G = g_ref[d, rows, :]
    g_mid = G[CHUNK // 2:CHUNK // 2 + 1, :]
    g_end = G[0:1, :] if rev else G[CHUNK - 1:CHUNK, :]
    e_g = jnp.exp(G)
    e_q = jnp.exp(G - g_mid)
    e_k = jnp.exp(g_mid - G)
    e_kend = jnp.exp(g_end - G)
    e_end = jnp.exp(g_end)
    nqk = GLA_HEADS * GLA_DK
    st = []
    for h in range(GLA_HEADS):
        sl = slice(h * GLA_DK, (h + 1) * GLA_DK)
        q = qk[:, sl] * (GLA_DK ** -0.5)
        k = qk[:, nqk + h * GLA_DK:nqk + (h + 1) * GLA_DK]
        st.append(dict(d=d, h=h, rows=rows, incl=incl,
                       v=v[:, h * GLA_DV:(h + 1) * GLA_DV],
                       qt=(q * e_q[:, sl]).astype(BF16), kt=(k * e_k[:, sl]).astype(BF16),
                       qg=(q * e_g[:, sl]).astype(BF16), ke=(k * e_kend[:, sl]).astype(BF16),
                       e_end=e_end[:, sl]))
    for s in st:
        s["a"] = jnp.where(s["incl"], _dot_nt(s["qt"], s["kt"]), 0.0).astype(BF16)
    for s in st:
        s["kv"] = _dot_tn(s["ke"], s["v"])
    return st


def _gla_recur(st, o_refs, s_ref):
    for s in st:
        s["s"] = s_ref[s["d"], s["h"]]
        s["o"] = _dot(s["a"], s["v"]) + _dot(s["qg"], s["s"].astype(BF16))
    for s in st:
        h = s["h"]
        o_refs[s["d"]][s["rows"], h * GLA_DV:(h + 1) * GLA_DV] = s["o"].astype(BF16)
        decay = jnp.broadcast_to(s["e_end"], (GLA_DK, GLA_DK)).T
        decay = jnp.concatenate([decay] * (GLA_DV // GLA_DK), axis=1)
        s_ref[s["d"], h] = decay * s["s"] + s["kv"]


def _gla_kernel(qkf, vf, smf, qkb, vb, smb, lrw, lrb, s0, of, ob, sfin, s_ref, g_ref, *, nchunk):
    i = pl.program_id(1)

    @pl.when(i == 0)
    def _():
        s_ref[...] = s0[...]

    _gla_gates(0, smf, lrw, lrb, g_ref)
    _gla_gates(1, smb, lrw, lrb, g_ref)
    local = lambda t: (_gla_local(0, t, qkf, vf, g_ref) + _gla_local(1, nchunk - 1 - t, qkb, vb, g_ref))
    pending = local(0)
    for t in range(nchunk):
        ahead = local(t + 1) if t + 1 < nchunk else None
        _gla_recur(pending, (of, ob), s_ref)
        pending = ahead

    @pl.when(i == pl.num_programs(1) - 1)
    def _():
        sfin[...] = s_ref[...]


def _gla_call(gqk, gv, sm, lrw_pad, lrb, s0, tb):
    b, l, _ = gqk.shape
    nblk = l // tb
    fwd = lambda bi, i: (bi, i, 0)
    bwd = lambda bi, i: (bi, nblk - 1 - i, 0)
    sspec = pl.BlockSpec((None, 2, GLA_HEADS, GLA_DK, GLA_DV), lambda bi, i: (bi, 0, 0, 0, 0))
    wide = GLA_HEADS * GLA_DV
    return pl.pallas_call(
        functools.partial(_gla_kernel, nchunk=tb // CHUNK),
        out_shape=[jax.ShapeDtypeStruct((b, l, wide), BF16),
                   jax.ShapeDtypeStruct((b, l, wide), BF16),
                   jax.ShapeDtypeStruct(s0.shape, F32)],
        grid=(b, nblk),
        in_specs=[pl.BlockSpec((None, tb, 2 * GLA_HEADS * GLA_DK), fwd),
                  pl.BlockSpec((None, tb, wide), fwd),
                  pl.BlockSpec((None, tb, LANES), fwd),
                  pl.BlockSpec((None, tb, 2 * GLA_HEADS * GLA_DK), bwd),
                  pl.BlockSpec((None, tb, wide), bwd),
                  pl.BlockSpec((None, tb, LANES), bwd),
                  _resident(lrw_pad.shape),
                  _resident(lrb.shape),
                  sspec],
        out_specs=[pl.BlockSpec((None, tb, wide), fwd),
                   pl.BlockSpec((None, tb, wide), bwd),
                   sspec],
        scratch_shapes=[pltpu.VMEM((2, GLA_HEADS, GLA_DK, GLA_DV), F32),
                        pltpu.VMEM((2, tb, GLA_HEADS * GLA_DK), F32)],
        compiler_params=_params(2),
        name="gla",
    )(gqk, gv, sm, gqk, gv, sm, lrw_pad, lrb, s0)


def _gdn_conv(x_ref, convw_ref, xp_ref, qkv_ref):
    rows = x_ref.shape[0]
    nq = GDN_HEADS * GDN_D
    width = 3 * nq
    zeros = jnp.zeros((CONV_PAD, width), F32)
    xp_ref[0:CONV_PAD, :] = zeros
    xp_ref[CONV_PAD + rows:CONV_PAD + rows + CONV_PAD, :] = zeros
    xp_ref[CONV_PAD:CONV_PAD + rows, :] = x_ref[...].astype(F32)
    acc = None
    for i in range(CONV_W):
        start = CONV_PAD + i - CONV_W // 2
        term = convw_ref[i:i + 1, :] * xp_ref[start:start + rows, :]
        acc = term if acc is None else acc + term
    xs = _silu(acc)
    for h in range(GDN_HEADS):
        qh = xs[:, h * GDN_D:(h + 1) * GDN_D]
        kh = xs[:, nq + h * GDN_D:nq + (h + 1) * GDN_D]
        qn = qh * (lax.rsqrt(jnp.sum(qh * qh, axis=-1, keepdims=True) + EPS) * GDN_D ** -0.5)
        kn = kh * lax.rsqrt(jnp.sum(kh * kh, axis=-1, keepdims=True) + EPS)
        qkv_ref[:, h * GDN_D:(h + 1) * GDN_D] = qn.astype(BF16)
        qkv_ref[:, nq + h * GDN_D:nq + (h + 1) * GDN_D] = kn.astype(BF16)
    qkv_ref[:, 2 * nq:] = xs[:, 2 * nq:].astype(BF16)


def _dot_exact(t, onehot):
    hi, mid, lo = _split3(t)
    return _dot(hi, onehot) + _dot(mid, onehot) + _dot(lo, onehot)


def _chunk_end(rev, c):
    return c * CHUNK if rev else (c + 1) * CHUNK - 1


def _gdn_gates(d, sm_ref, alog_ref, dtb_ref, e64_ref, e128_ref, gi_ref, gt_ref, bt_ref, ge_ref):
    rev = d == 1
    rows = sm_ref.shape[0]
    sm = sm_ref[...]
    gate = -jnp.exp(alog_ref[...]) * _softplus(sm + dtb_ref[...])
    tri = jnp.where(_tri(rev), 1.0, 0.0).astype(BF16)
    nchunk = rows // CHUNK
    gcum = jnp.concatenate([_dot_exact_lhs(tri, gate[c * CHUNK:(c + 1) * CHUNK, :]) for c in range(nchunk)],
                           axis=0)
    gi_ref[d] = _dot_exact(gcum, e64_ref[d])
    gt_ref[d] = gcum.T
    bt_ref[d] = _sigmoid(sm).T
    ends = [gcum[_chunk_end(rev, c):_chunk_end(rev, c) + 1, :] for c in range(nchunk)]
    if nchunk < ge_ref.shape[1]:
        ends.append(jnp.zeros((ge_ref.shape[1] - nchunk, LANES), F32))
    ge_ref[d] = _dot_exact(jnp.concatenate(ends, axis=0), e128_ref[d])


def _dot_exact_lhs(onehot, t):
    hi, mid, lo = _split3(t)
    return _dot(onehot, hi) + _dot(onehot, mid) + _dot(onehot, lo)


def _bd_wide(m):
    low = lax.broadcasted_iota(jnp.int32, (CHUNK, LANES), 1) < CHUNK
    zero = jnp.zeros((CHUNK, LANES), m.dtype)
    rows = []
    for a in range(GDN_GROUP):
        tile = m[:, (a // 2) * LANES:(a // 2 + 1) * LANES]
        kept = jnp.where(low, tile, zero) if a % 2 == 0 else jnp.where(low, zero, tile)
        rows.append(jnp.concatenate([kept, zero] if a < 2 else [zero, kept], axis=1))
    return jnp.concatenate(rows, axis=0)


def _bd_heads(t):
    zero = jnp.zeros((CHUNK, GDN_D), t.dtype)
    return jnp.concatenate(
        [jnp.concatenate([t[:, a * GDN_D:(a + 1) * GDN_D] if b == a else zero for b in range(GDN_GROUP)], axis=1)
         for a in range(GDN_GROUP)], axis=0)


def _run_stages(*stage_generators):
    active = list(stage_generators)
    while active:
        for g in list(active):
            if next(g, StopIteration) is StopIteration:
                active.remove(g)


def _gdn_local(st, groups, qkv_refs, gi_ref, gt_ref, bt_ref, ge_ref):
    nq = GDN_HEADS * GDN_D
    span = GDN_GROUP * GDN_D
    row = lax.broadcasted_iota(jnp.int32, (CHUNK, WIDE), 0)
    col = lax.broadcasted_iota(jnp.int32, (CHUNK, WIDE), 1) & (CHUNK - 1)
    eye = row == col
    eye_f = jnp.where(eye, 1.0, 0.0)

    def row_of(t_ref, base, d, hg, rs):
        lane = lambda a: base + GDN_HEADS * d + GDN_GROUP * hg + a
        return jnp.concatenate([t_ref[d, lane(a):lane(a) + 1, rs] for a in range(GDN_GROUP)], axis=1)

    for d, c, hg in groups:
        rev = d == 1
        rs = slice(c * CHUNK, (c + 1) * CHUNK)
        end = _chunk_end(rev, c)
        qkv_ref = qkv_refs[d]
        wl = slice(hg * WIDE, (hg + 1) * WIDE)
        st.append(dict(
            d=d, hg=hg, rs=rs,
            strict=(col > row) if rev else (col < row),
            q=qkv_ref[rs, hg * span:(hg + 1) * span],
            k=qkv_ref[rs, nq + hg * span:nq + (hg + 1) * span],
            v=qkv_ref[rs, 2 * nq + hg * span:2 * nq + (hg + 1) * span],
            gi=gi_ref[d, rs, wl],
            gj=row_of(gt_ref, SM_A, d, hg, rs),
            bj=row_of(bt_ref, SM_B, d, hg, rs),
            g_end=gi_ref[d, end:end + 1, wl],
            s_decay=ge_ref[d, c:c + 1, hg * span:(hg + 1) * span]))
    for s in st:
        s["bdk"] = _bd_heads(s["k"])
        s["qkk"] = _dot_nt(jnp.concatenate([s["q"], s["k"]], axis=0), s["bdk"])
    yield
    for s in st:
        dec = jnp.exp(jnp.where(s["strict"], s["gi"] - s["gj"], -1e30))
        s["dec"] = dec
        x = -(s["qkk"][CHUNK:] * dec * s["bj"])
        s["x"] = x
        s["p"] = eye_f + jnp.where((row >> 1) == (col >> 1), x, 0.0)
    for level in range(1, 6):
        sibling = ((row >> level) ^ 1) == (col >> level)
        for s in st:
            s["pb"] = s["p"].astype(BF16)
            s["t"] = _dot(jnp.where(sibling, s["x"], 0.0).astype(BF16), _bd_wide(s["pb"]))
        yield
        for s in st:
            s["p"] = s["p"] + _dot(s["pb"], _bd_wide(s["t"].astype(BF16)))
        yield
    for s in st:
        e_gj = jnp.exp(s["gj"])
        s["rv"] = _dot(s["p"].astype(BF16), _bd_heads(s["v"]))
        s["w"] = _dot((s["p"] * e_gj).astype(BF16), s["bdk"]).astype(BF16)
        s["e"] = jnp.where(eye, e_gj, 0.0).astype(BF16)
        a_beta = s["qkk"][:CHUNK] * (s["dec"] + eye_f) * s["bj"]
        kscale = jnp.where(eye, jnp.exp(s["g_end"] - s["gj"]) * s["bj"], 0.0)
        s["ak"] = jnp.concatenate([a_beta, kscale], axis=0).astype(BF16)
    yield


def _gdn_recur(st, o_refs, s_ref):
    pair = 2 * GDN_D
    zero = jnp.zeros((GDN_D, GDN_D), BF16)
    for s in st:
        s["s"] = s_ref[s["d"], s["hg"]]
        sb = s["s"].astype(BF16)
        out = []
        for p in range(GDN_GROUP // 2):
            lanes = slice(p * pair, (p + 1) * pair)
            s_a, s_b = sb[:, p * pair:p * pair + GDN_D], sb[:, p * pair + GDN_D:(p + 1) * pair]
            bd = jnp.concatenate([jnp.concatenate([s_a, zero], axis=1),
                                  jnp.concatenate([zero, s_b], axis=1)], axis=0)
            out.append(_dot(jnp.concatenate([s["w"][:, lanes], s["q"][:, lanes]], axis=0), bd))
        s["wqs"] = jnp.concatenate(out, axis=1)
    yield
    for s in st:
        vhat = (s["rv"] - s["wqs"][:CHUNK]).astype(BF16)
        s["av"] = _dot(s["ak"], _bd_heads(vhat))
        s["eq"] = _dot(s["e"], _bd_heads(s["wqs"][CHUNK:].astype(BF16)))
    yield
    for s in st:
        span = GDN_GROUP * GDN_D
        o_refs[s["d"]][s["rs"], s["hg"] * span:(s["hg"] + 1) * span] = (s["eq"] + s["av"][:CHUNK]).astype(BF16)
        k_rows = jnp.concatenate([s["k"][:, a * GDN_D:(a + 1) * GDN_D] for a in range(GDN_GROUP)], axis=0)
        update = _dot_tn(k_rows, _bd_heads(s["av"][CHUNK:].astype(BF16)))
        s_ref[s["d"], s["hg"]] = jnp.exp(s["s_decay"]) * s["s"] + update
    yield


def _gdn_scan(qkv_refs, smf, smb, alog, dtb, e64, e128, s0, of, ob, sfin, s_ref, gi_ref, gt_ref, bt_ref, ge_ref):
    i = pl.program_id(1)

    @pl.when(i == 0)
    def _():
        s_ref[...] = s0[...]

    _gdn_gates(0, smf, alog, dtb, e64, e128, gi_ref, gt_ref, bt_ref, ge_ref)
    _gdn_gates(1, smb, alog, dtb, e64, e128, gi_ref, gt_ref, bt_ref, ge_ref)
    nchunk = smf.shape[0] // CHUNK
    ngroup = GDN_HEADS // GDN_GROUP
    step_groups = lambda t: ([(0, t, hg) for hg in range(ngroup)]
                             + [(1, nchunk - 1 - t, hg) for hg in range(ngroup)])
    steps = [[] for _ in range(nchunk)]
    local = lambda ts: [_gdn_local(steps[t], step_groups(t), qkv_refs, gi_ref, gt_ref, bt_ref, ge_ref) for t in ts]

    def recur(ts):
        for t in ts:
            yield from _gdn_recur(steps[t], (of, ob), s_ref)

    early, late = range(0, (nchunk + 1) // 2), range((nchunk + 1) // 2, nchunk)
    _run_stages(*local(early))
    _run_stages(recur(early), *local(late))
    _run_stages(recur(late))

    @pl.when(i == pl.num_programs(1) - 1)
    def _():
        sfin[...] = s_ref[...]


def _gdn_kernel(qkvf, smf, qkvb, smb, alog, dtb, e64, e128, s0, of, ob, sfin, s_ref, gi_ref, gt_ref, bt_ref, ge_ref):
    _gdn_scan((qkvf, qkvb), smf, smb, alog, dtb, e64, e128, s0, of, ob, sfin, s_ref, gi_ref, gt_ref, bt_ref, ge_ref)


def _gdn_raw_kernel(xf, smf, xb, smb, convw, alog, dtb, e64, e128, s0, of, ob, sfin,
                    s_ref, gi_ref, gt_ref, bt_ref, ge_ref, xp_ref, qkv_ref):
    _gdn_conv(xf, convw, xp_ref, qkv_ref.at[0])
    _gdn_conv(xb, convw, xp_ref, qkv_ref.at[1])
    _gdn_scan((qkv_ref.at[0], qkv_ref.at[1]), smf, smb, alog, dtb, e64, e128, s0, of, ob, sfin,
              s_ref, gi_ref, gt_ref, bt_ref, ge_ref)


def _gdn_state_shape(b):
    return (b, 2, GDN_HEADS // GDN_GROUP, GDN_D, GDN_GROUP * GDN_D)


def _gdn_call(qkv, sm, alog_row, dtb_row, e64, e128, s0, conv_w=None):
    b, ncol, rows, width = qkv.shape
    wide = GDN_HEADS * GDN_D
    nchunk = rows // CHUNK
    fwd = lambda bi, i: (bi, i, 0, 0)
    bwd = lambda bi, i: (bi, ncol - 1 - i, 0, 0)
    slab = lambda w, index: pl.BlockSpec((None, None, rows, w), index)
    sspec = pl.BlockSpec((None,) + s0.shape[1:], lambda bi, i: (bi, 0, 0, 0, 0))
    params = [alog_row, dtb_row, e64, e128]
    scratch = [pltpu.VMEM(s0.shape[1:], F32),
               pltpu.VMEM((2, rows, GDN_HEADS * CHUNK), F32),
               pltpu.VMEM((2, LANES, rows), F32),
               pltpu.VMEM((2, LANES, rows), F32),
               pltpu.VMEM((2, 8, wide), F32)]
    assert nchunk <= 8
    body = _gdn_kernel
    if conv_w is not None:
        params = [conv_w] + params
        scratch += [pltpu.VMEM((rows + 2 * CONV_PAD, width), F32),
                    pltpu.VMEM((2, rows, width), BF16)]
        body = _gdn_raw_kernel
    return pl.pallas_call(
        body,
        out_shape=[jax.ShapeDtypeStruct((b, ncol, rows, wide), BF16),
                   jax.ShapeDtypeStruct((b, ncol, rows, wide), BF16),
                   jax.ShapeDtypeStruct(s0.shape, F32)],
        grid=(b, ncol),
        in_specs=[slab(width, fwd), slab(LANES, fwd), slab(width, bwd), slab(LANES, bwd)]
                 + [_resident(p.shape) for p in params] + [sspec],
        out_specs=[slab(wide, fwd), slab(wide, bwd), sspec],
        scratch_shapes=scratch,
        compiler_params=_params(2),
        name="gdn",
    )(qkv, sm, qkv, sm, *params, s0)


def _head_norm(o, heads, gain):
    dv = o.shape[-1] // heads
    parts = []
    for h in range(heads):
        oh = o[:, h * dv:(h + 1) * dv]
        parts.append(oh * lax.rsqrt(jnp.mean(oh * oh, axis=-1, keepdims=True) + EPS))
    return jnp.concatenate(parts, axis=-1) * gain


def _tail_kernel(x_ref, glaf, glab, gz, gdnf, gdnb, dz, og, gate1, shift2, scale2, gate2,
                 gla_gain, gdn_gain, n2w, fw, w_out, w_gate, w_up, w_down, out_ref, merged_ref):
    d = x_ref.shape[-1]

    @pl.when(pl.program_id(0) == 0)
    def _():
        merged_ref[...] = jnp.zeros_like(merged_ref)

    def merge(r):
        f32 = lambda ref: ref[r, :].astype(F32)
        y_gla = _head_norm(f32(glaf) + f32(glab), GLA_HEADS, gla_gain[...]) * f32(gz)
        y_gdn = _head_norm(f32(gdnf) + f32(gdnb), GDN_HEADS, gdn_gain[...]) * f32(dz)
        gates = f32(og)
        merged_ref[r, :] = (gates[:, :d] * y_gla + gates[:, d:] * y_gdn).astype(BF16)

    tm = x_ref.shape[0]
    parts = [slice(0, tm // 2), slice(tm // 2, tm)]
    quarters = [slice(q * tm // 4, (q + 1) * tm // 4) for q in range(4)]
    x1 = [x_ref[r, :] + gate1[...] * _dot(merged_ref[r, :], w_out[...]) for r in parts]
    merge(quarters[0])
    h2 = [(_rms(t, n2w[...]) * (1.0 + scale2[...]) + shift2[...]).astype(BF16) for t in x1]
    merge(quarters[1])
    hidden = [(_silu(_dot(t, w_gate[...])) * _dot(t, w_up[...])).astype(BF16) for t in h2]
    merge(quarters[2])
    x2 = [a + gate2[...] * _dot(t, w_down[...]) for a, t in zip(x1, hidden)]
    merge(quarters[3])
    for r, t in zip(parts, x2):
        out_ref[r, :] = _rms(t, fw[...])


def _tail_call(x2d, glaf, glab, gz, gdnf, gdnb, dz, og, mod3, gla_gain, gdn_gain, n2w, fw,
               w_out, w_gate, w_up, w_down, tokens_per_mod, tm):
    n, d = x2d.shape
    ntile = n // tm
    merge_tile = lambda s: jnp.minimum(s, ntile - 1)
    chain_tile = lambda s: jnp.maximum(s - 1, 0)
    mod_row = lambda s: (chain_tile(s) * tm) // tokens_per_mod
    mix = lambda w: pl.BlockSpec((tm, w), lambda s: (merge_tile(s), 0))
    tok = pl.BlockSpec((tm, d), lambda s: (chain_tile(s), 0))
    modspec = lambda j: pl.BlockSpec((None, 1, d), lambda s: (mod_row(s), 0, j))
    return pl.pallas_call(
        _tail_kernel,
        out_shape=jax.ShapeDtypeStruct((n, d), F32),
        grid=(ntile + 1,),
        in_specs=[tok] + [mix(d)] * 6 + [mix(2 * d)]
                 + [modspec(2), modspec(3), modspec(4), modspec(5)]
                 + [_resident(a.shape) for a in (gla_gain, gdn_gain, n2w, fw, w_out, w_gate, w_up, w_down)],
        out_specs=tok,
        scratch_shapes=[pltpu.VMEM((tm, d), BF16)],
        compiler_params=_params(1),
        name="tail",
    )(x2d, glaf, glab, gz, gdnf, gdnb, dz, og, mod3, mod3, mod3, mod3,
      gla_gain, gdn_gain, n2w, fw, w_out, w_gate, w_up, w_down)


def _split_w_in(w_in):
    d = w_in.shape[0]
    gq, gk, gv, gz, lrf, lrb, dq, dk, dv, dz, ab, p11, p12 = jnp.split(
        w_in, np.cumsum([512, 512, 1024, 1024, 16, 16, 1024, 1024, 1024, 1024, 32, 1024]).tolist(), axis=1)
    pad = jnp.zeros((d, LANES - 2 * GLA_LOWRANK - 4 * GDN_HEADS), w_in.dtype)
    cat = lambda *parts: jnp.concatenate(parts, axis=1).astype(BF16)
    return dict(gqk=cat(gq, gk), gv=cat(gv), gz=cat(gz), dqkv=cat(dq, dk, dv), dz=cat(dz),
                og=cat(p11, p12), sm=cat(lrf, lrb, ab, pad))


def _gdn_expand():
    e64 = np.zeros((2, LANES, GDN_HEADS * CHUNK), np.float32)
    e128 = np.zeros((2, LANES, GDN_HEADS * GDN_D), np.float32)
    for d in range(2):
        for h in range(GDN_HEADS):
            e64[d, SM_A + GDN_HEADS * d + h, h * CHUNK:(h + 1) * CHUNK] = 1.0
            e128[d, SM_A + GDN_HEADS * d + h, h * GDN_D:(h + 1) * GDN_D] = 1.0
    return jnp.asarray(e64, dtype=BF16), jnp.asarray(e128, dtype=BF16)


def kernel(x, c, ctx, c_ctx, w_mod, b_mod, norm1_w, norm2_w, w_in, gla_lr_w, gla_lr_b, gla_norm_w, gdn_conv_w, gdn_a_log, gdn_dt_bias, gdn_norm_w, w_out, ffn_w_gate, ffn_w_up, ffn_w_down, final_norm_w):
    b, seq, d = x.shape
    ctx_len = ctx.shape[1]
    depth = w_mod.shape[0]
    assert depth == 1, "only the single-layer block is implemented"
    assert d == GLA_HEADS * GLA_DV == GDN_HEADS * GDN_D
    rows = seq // GRID_W
    assert seq % GRID_W == 0 and rows % CHUNK == 0 and ctx_len % CHUNK == 0
    layer = 0

    mod_rows = -(-(b + 1) // 8) * 8
    cc = jnp.zeros((mod_rows, d), F32).at[:b].set(c).at[b].set(c_ctx)
    mod = _mod_call(cc, w_mod[layer], b_mod[layer][None, :])
    mod_l = mod[:b, None, :]
    mod_c = mod[b:b + 1, None, :]

    w = _split_w_in(w_in[layer])
    n1w = norm1_w[layer][None, :]
    conv_w = gdn_conv_w[layer]
    tm = 256
    lat = (("gqk", None), ("gv", None), ("gz", _silu), ("dz", _silu), ("og", _sigmoid), ("sm", None))
    gqk_l, gv_l, gz_l, dz_l, og_l, sm_l = _inproj_call(
        x.reshape(b * seq, d), mod_l, n1w, [w[k] for k, _ in lat], [a for _, a in lat], seq, tm)
    gqk_c, gv_c, dqkv_c, sm_c = _inproj_call(
        ctx.reshape(b * ctx_len, d), mod_c, n1w, [w[k] for k in ("gqk", "gv", "dqkv", "sm")], [None] * 4,
        b * ctx_len, tm)
    qkv_l, smg_l = _gproj_call(x.reshape(b, rows, GRID_W, d), mod_l, n1w,
                               jnp.concatenate([w["dqkv"], w["sm"]], axis=1), conv_w)

    lrw_pad = jnp.zeros((2, LANES, GLA_HEADS * GLA_DK), F32)
    lrw_pad = lrw_pad.at[0, SM_LRF:SM_LRF + GLA_LOWRANK].set(gla_lr_w[layer, 0])
    lrw_pad = lrw_pad.at[1, SM_LRB:SM_LRB + GLA_LOWRANK].set(gla_lr_w[layer, 1])
    lrb = gla_lr_b[layer][:, None, :]
    r3 = lambda t, l: t.reshape(b, l, t.shape[-1])
    s0_gla = jnp.zeros((b, 2, GLA_HEADS, GLA_DK, GLA_DV), F32)
    _, _, s_gla = _gla_call(r3(gqk_c, ctx_len), r3(gv_c, ctx_len), r3(sm_c, ctx_len), lrw_pad, lrb, s0_gla, ctx_len)
    gla_f, gla_b, _ = _gla_call(r3(gqk_l, seq), r3(gv_l, seq), r3(sm_l, seq), lrw_pad, lrb, s_gla, 256)

    lane_params = lambda p: jnp.zeros((1, LANES), F32).at[0, SM_A:SM_A + 2 * GDN_HEADS].set(p[layer].reshape(-1))
    alog_row, dtb_row = lane_params(gdn_a_log), lane_params(gdn_dt_bias)
    e64, e128 = _gdn_expand()
    s0_gdn = jnp.zeros(_gdn_state_shape(b), F32)
    one_col = lambda t: t.reshape(b, 1, ctx_len, t.shape[-1])
    _, _, s_gdn = _gdn_call(one_col(dqkv_c), one_col(sm_c), alog_row, dtb_row, e64, e128, s0_gdn, conv_w=conv_w)
    merge = lambda t: t.reshape(b, GRID_W // GDN_STEP_COLS, GDN_STEP_COLS * rows, t.shape[-1])
    gdn_f, gdn_b, _ = _gdn_call(merge(qkv_l), merge(smg_l), alog_row, dtb_row, e64, e128, s_gdn)
    gdn_f, gdn_b = (t.reshape(b, GRID_W, rows, d) for t in (gdn_f, gdn_b))

    flat = lambda t: t.reshape(b * seq, d)
    raster = lambda t: t.transpose(0, 2, 1, 3).reshape(b * seq, d)
    out = _tail_call(
        x.reshape(b * seq, d), flat(gla_f), flat(gla_b), gz_l, raster(gdn_f), raster(gdn_b), dz_l, og_l, mod_l,
        gla_norm_w[layer].reshape(1, d), gdn_norm_w[layer].reshape(1, d),
        norm2_w[layer][None, :], final_norm_w[None, :],
        w_out[layer].astype(BF16), ffn_w_gate[layer].astype(BF16), ffn_w_up[layer].astype(BF16),
        ffn_w_down[layer].astype(BF16), seq, tm)
    return out.reshape(b, seq, d)
```
